```python
import jax, jax.numpy as jnp
from jax import lax
import numpy as np

D_MODEL = 2048
BATCH = 2
SEQ = 8192
DEPTH = 1

CHUNK = 128
A_HEADS = 8
A_WIDTH = D_MODEL
A_HEAD_DIM = A_WIDTH // A_HEADS
B_GROUPS = 16
B_WIDTH = D_MODEL
CONV_WIDTH = 3
N_BRANCH = 2
DN_ALPHA = (2.0 * DEPTH) ** 0.25
DN_BETA = (8.0 * DEPTH) ** -0.25
LN_EPS = 1e-5
IN_WIDTHS = (A_WIDTH, A_WIDTH, A_WIDTH, B_WIDTH, B_WIDTH, B_WIDTH, B_WIDTH, D_MODEL, D_MODEL)
IN_COLS = 3 * A_WIDTH + 4 * B_WIDTH + N_BRANCH * D_MODEL

kernel_name = "hybrid_gated_sgu_shortconv_deepnorm"


def _split_points():
    pts, acc = [], 0
    for w in IN_WIDTHS[:-1]:
        acc += w
        pts.append(acc)
    return pts


def layer_norm(x, g, b):
    xf = x.astype(jnp.float32)
    mu = jnp.mean(xf, axis=-1, keepdims=True)
    var = jnp.mean(jnp.square(xf - mu), axis=-1, keepdims=True)
    y = (xf - mu) * lax.rsqrt(var + LN_EPS)
    return (y * g.astype(jnp.float32) + b.astype(jnp.float32)).astype(x.dtype)


def chunked_sgu(u, v, ln_g, ln_b, w_s, b_s):
    bsz, s, _ = u.shape
    nc = s // CHUNK
    u = jax.nn.gelu(u).reshape(bsz, nc, CHUNK, A_HEADS, A_HEAD_DIM)
    v = jax.nn.gelu(v).reshape(bsz, nc, CHUNK, A_HEADS, A_HEAD_DIM)
    v = layer_norm(v, ln_g.reshape(A_HEADS, A_HEAD_DIM), ln_b.reshape(A_HEADS, A_HEAD_DIM))
    causal = jnp.tril(jnp.ones((CHUNK, CHUNK), dtype=bool))
    w = jnp.where(causal[None], w_s, jnp.zeros_like(w_s))
    mixed = jnp.einsum('hts,bcshd->bcthd', w, v)
    mixed = mixed + jnp.transpose(b_s)[None, None, :, :, None]
    return (u * mixed).reshape(bsz, s, A_WIDTH)


def short_gated_conv(xb, cb, bb, conv_w, conv_b):
    s = xb.shape[1]
    h = cb * xb
    hp = jnp.pad(h, ((0, 0), (CONV_WIDTH - 1, 0), (0, 0)))
    conv = conv_b + conv_w[0] * hp[:, 0:s, :]
    for k in range(1, CONV_WIDTH):
        conv = conv + conv_w[k] * hp[:, k:k + s, :]
    return bb * conv


def setup_inputs(seed: int = 0) -> dict:
    key = jax.random.key(seed)
    ks = jax.random.split(key, 16)
    nrm = jax.random.normal
    x = nrm(ks[0], (BATCH, SEQ, D_MODEL), jnp.float32)
    w_in = nrm(ks[1], (DEPTH, D_MODEL, IN_COLS), jnp.float32) * D_MODEL ** -0.5
    b_gate = 0.02 * nrm(ks[2], (DEPTH, N_BRANCH * D_MODEL), jnp.float32)
    ln_v_g = 1.0 + 0.02 * nrm(ks[3], (DEPTH, A_WIDTH), jnp.float32)
    ln_v_b = 0.02 * nrm(ks[4], (DEPTH, A_WIDTH), jnp.float32)
    w_s = nrm(ks[5], (DEPTH, A_HEADS, CHUNK, CHUNK), jnp.float32) * (0.5 * CHUNK ** -0.5)
    b_s = 1.0 + 0.02 * nrm(ks[6], (DEPTH, A_HEADS, CHUNK), jnp.float32)
    conv_w = nrm(ks[7], (DEPTH, CONV_WIDTH, B_WIDTH), jnp.float32) * CONV_WIDTH ** -0.5
    conv_b = 0.02 * nrm(ks[8], (DEPTH, B_WIDTH), jnp.float32)
    w_oa = nrm(ks[9], (DEPTH, A_WIDTH, D_MODEL), jnp.float32) * (A_WIDTH ** -0.5 * DN_BETA)
    w_ob = nrm(ks[10], (DEPTH, B_WIDTH, D_MODEL), jnp.float32) * (B_WIDTH ** -0.5 * DN_BETA)
    w_out = nrm(ks[11], (DEPTH, D_MODEL, D_MODEL), jnp.float32) * (D_MODEL ** -0.5 * DN_BETA)
    ln_g = 1.0 + 0.02 * nrm(ks[12], (DEPTH, D_MODEL), jnp.float32)
    ln_b = 0.02 * nrm(ks[13], (DEPTH, D_MODEL), jnp.float32)
    return {"x": x, "w_in": w_in, "b_gate": b_gate, "ln_v_g": ln_v_g, "ln_v_b": ln_v_b,
            "w_s": w_s, "b_s": b_s, "conv_w": conv_w, "conv_b": conv_b,
            "w_oa": w_oa, "w_ob": w_ob, "w_out": w_out, "ln_g": ln_g, "ln_b": ln_b}


def reference(x, w_in, b_gate, ln_v_g, ln_v_b, w_s, b_s, conv_w, conv_b,
              w_oa, w_ob, w_out, ln_g, ln_b):
    splits = _split_points()
    for l in range(DEPTH):
        p = jnp.einsum('bsd,dc->bsc', x, w_in[l])
        ua, va, za, xb, cb, bb, zb, ga, gb = jnp.split(p, splits, axis=-1)
        ya = chunked_sgu(ua, va, ln_v_g[l], ln_v_b[l], w_s[l], b_s[l]) * jax.nn.silu(za)
        yb = short_gated_conv(xb, cb, bb, conv_w[l], conv_b[l]) * jax.nn.silu(zb)
        gate_a = jax.nn.sigmoid(ga + b_gate[l, :D_MODEL])
        gate_b = jax.nn.sigmoid(gb + b_gate[l, D_MODEL:])
        merged = (gate_a * jnp.einsum('bse,ed->bsd', ya, w_oa[l])
                  + gate_b * jnp.einsum('bse,ed->bsd', yb, w_ob[l]))
        out = jnp.einsum('bsd,de->bse', merged, w_out[l])
        x = layer_norm(DN_ALPHA * x + out, ln_g[l], ln_b[l])
    return x
```

```python
import functools

import jax
import jax.numpy as jnp
from jax import lax
from jax.experimental import pallas as pl
from jax.experimental.pallas import tpu as pltpu

D_MODEL = 2048
CHUNK = 128
HEAD_DIM = 256
N_HEADS = D_MODEL // HEAD_DIM
N_GROUPS = 9
CONV_WIDTH = 3
DN_ALPHA = 2.0 ** 0.25
LN_EPS = 1e-5

TOKEN_TILE = 512
VMEM_LIMIT_BYTES = 56 * 1024 * 1024

_F32 = jnp.float32
_BF16 = jnp.bfloat16


def _dot(a, b):
    return jnp.dot(a, b, preferred_element_type=_F32)


def _fused_kernel(x_ref, wu_ref, wv_ref, wza_ref, wxb_ref, wcb_ref, wbb_ref, wzb_ref,
                  lnvg_ref, lnvb_ref, ws_ref, bs_ref, cw_ref, cbias_ref,
                  woa_ref, wob_ref, wga_ref, wgb_ref, bga_ref, bgb_ref, wout_ref,
                  lng_ref, lnb_ref,
                  o_ref,
                  xbf_ref, ya_ref, yb_ref, hcarry_ref,
                  *, tiles_per_seq):
    i = pl.program_id(0)
    s = pl.program_id(1)
    tm = x_ref.shape[0]

    @pl.when(s == 0)
    def _cast_x():
        xbf_ref[...] = x_ref[...].astype(_BF16)

    @pl.when(s < N_HEADS)
    def _phase1():
        xb16 = xbf_ref[...]

        v = jax.nn.gelu(_dot(xb16, wv_ref[...]))
        mu = jnp.mean(v, axis=-1, keepdims=True)
        vc = v - mu
        var = jnp.mean(vc * vc, axis=-1, keepdims=True)
        vn = vc * lax.rsqrt(var + LN_EPS) * lnvg_ref[...] + lnvb_ref[...]
        vn16 = vn.astype(_BF16)
        row = lax.broadcasted_iota(jnp.int32, (CHUNK, CHUNK), 0)
        col = lax.broadcasted_iota(jnp.int32, (CHUNK, CHUNK), 1)
        w_mix = jnp.where(row >= col, ws_ref[0], 0.0).astype(_BF16)
        bias_col = bs_ref[0]
        mixed = jnp.concatenate(
            [_dot(w_mix, vn16[c * CHUNK:(c + 1) * CHUNK, :]) + bias_col
             for c in range(tm // CHUNK)], axis=0)
        u = jax.nn.gelu(_dot(xb16, wu_ref[...]))
        za = jax.nn.silu(_dot(xb16, wza_ref[...]))
        ya_val = (u * mixed * za).astype(_BF16)

        h = _dot(xb16, wcb_ref[...]) * _dot(xb16, wxb_ref[...])
        @pl.when((i % tiles_per_seq) == 0)
        def _reset_history():
            hcarry_ref[s] = jnp.zeros((8, HEAD_DIM), _F32)

        prev = hcarry_ref[s]
        hcarry_ref[s] = h[tm - 8:, :]
        rid = lax.broadcasted_iota(jnp.int32, h.shape, 0)
        h1 = jnp.where(rid == 0, prev[7:8, :], pltpu.roll(h, 1, 0))
        h2 = jnp.where(rid == 0, prev[6:7, :],
                       jnp.where(rid == 1, prev[7:8, :], pltpu.roll(h, 2, 0)))
        conv = cbias_ref[...] + cw_ref[0:1, :] * h2 + cw_ref[1:2, :] * h1 + cw_ref[2:3, :] * h
        bb = _dot(xb16, wbb_ref[...])
        zb = jax.nn.silu(_dot(xb16, wzb_ref[...]))
        yb_val = (bb * conv * zb).astype(_BF16)

        for jj in range(N_HEADS):
            @pl.when(s == jj)
            def _store(jj=jj):
                ya_ref[:, jj * HEAD_DIM:(jj + 1) * HEAD_DIM] = ya_val
                yb_ref[:, jj * HEAD_DIM:(jj + 1) * HEAD_DIM] = yb_val

    @pl.when(s >= N_HEADS)
    def _phase2():
        n = s - N_HEADS
        xb16 = xbf_ref[...]
        a = _dot(ya_ref[...], woa_ref[...])
        b = _dot(yb_ref[...], wob_ref[...])
        ga = jax.nn.sigmoid(_dot(xb16, wga_ref[...]) + bga_ref[...])
        gb = jax.nn.sigmoid(_dot(xb16, wgb_ref[...]) + bgb_ref[...])
        merged = (ga * a + gb * b).astype(_BF16)
        contrib = _dot(merged, wout_ref[...])

        @pl.when(n == 0)
        def _init():
            o_ref[...] = contrib

        @pl.when(n > 0)
        def _acc():
            o_ref[...] += contrib

        @pl.when(n == N_HEADS - 1)
        def _post_norm():
            y = DN_ALPHA * x_ref[...] + o_ref[...]
            mu = jnp.mean(y, axis=-1, keepdims=True)
            yc = y - mu
            var = jnp.mean(yc * yc, axis=-1, keepdims=True)
            o_ref[...] = yc * lax.rsqrt(var + LN_EPS) * lng_ref[...] + lnb_ref[...]


def _layer(x2d, w_in_bf, b_gate, ln_v_g, ln_v_b, w_s, b_s, conv_w, conv_b,
           w_oa_bf, w_ob_bf, w_out_bf, ln_g, ln_b, *, seq_len):
    m = x2d.shape[0]
    tm = TOKEN_TILE
    assert m % tm == 0 and seq_len % tm == 0 and tm % CHUNK == 0
    n_tiles = m // tm
    last = N_HEADS - 1

    def head_of(s):
        return jnp.minimum(s, last)

    def col_of(s):
        return jnp.maximum(s - N_HEADS, 0)

    def w_in_head(g):
        return pl.BlockSpec((D_MODEL, HEAD_DIM), lambda i, s, g=g: (0, g * N_HEADS + head_of(s)))

    def w_in_col(g):
        return pl.BlockSpec((D_MODEL, HEAD_DIM), lambda i, s, g=g: (0, g * N_HEADS + col_of(s)))

    head_vec = pl.BlockSpec((1, HEAD_DIM), lambda i, s: (0, head_of(s)))
    in_specs = [
        pl.BlockSpec((tm, D_MODEL), lambda i, s: (i, 0)),
        w_in_head(0), w_in_head(1), w_in_head(2),
        w_in_head(3), w_in_head(4), w_in_head(5), w_in_head(6),
        head_vec, head_vec,
        pl.BlockSpec((1, CHUNK, CHUNK), lambda i, s: (head_of(s), 0, 0)),
        pl.BlockSpec((1, CHUNK, 1), lambda i, s: (head_of(s), 0, 0)),
        pl.BlockSpec((CONV_WIDTH, HEAD_DIM), lambda i, s: (0, head_of(s))),
        head_vec,
        pl.BlockSpec((D_MODEL, HEAD_DIM), lambda i, s: (0, col_of(s))),
        pl.BlockSpec((D_MODEL, HEAD_DIM), lambda i, s: (0, col_of(s))),
        w_in_col(7), w_in_col(8),
        pl.BlockSpec((1, HEAD_DIM), lambda i, s: (0, col_of(s))),
        pl.BlockSpec((1, HEAD_DIM), lambda i, s: (0, N_HEADS + col_of(s))),
        pl.BlockSpec((HEAD_DIM, D_MODEL), lambda i, s: (col_of(s), 0)),
        pl.BlockSpec((1, D_MODEL), lambda i, s: (0, 0)),
        pl.BlockSpec((1, D_MODEL), lambda i, s: (0, 0)),
    ]
    kern = functools.partial(_fused_kernel, tiles_per_seq=seq_len // tm)
    return pl.pallas_call(
        kern,
        grid=(n_tiles, 2 * N_HEADS),
        in_specs=in_specs,
        out_specs=pl.BlockSpec((tm, D_MODEL), lambda i, s: (i, 0)),
        out_shape=jax.ShapeDtypeStruct((m, D_MODEL), _F32),
        scratch_shapes=[
            pltpu.VMEM((tm, D_MODEL), _BF16),
            pltpu.VMEM((tm, D_MODEL), _BF16),
            pltpu.VMEM((tm, D_MODEL), _BF16),
            pltpu.VMEM((N_HEADS, 8, HEAD_DIM), _F32),
        ],
        compiler_params=pltpu.CompilerParams(
            dimension_semantics=("arbitrary", "arbitrary"),
            vmem_limit_bytes=VMEM_LIMIT_BYTES),
        name="fused_layer",
    )(x2d,
      w_in_bf, w_in_bf, w_in_bf, w_in_bf, w_in_bf, w_in_bf, w_in_bf,
      ln_v_g, ln_v_b, w_s, b_s, conv_w, conv_b,
      w_oa_bf, w_ob_bf, w_in_bf, w_in_bf, b_gate, b_gate, w_out_bf,
      ln_g, ln_b)


def kernel(x, w_in, b_gate, ln_v_g, ln_v_b, w_s, b_s, conv_w, conv_b, w_oa, w_ob, w_out, ln_g, ln_b):
    bsz, seq, d = x.shape
    depth = w_in.shape[0]
    x2d = x.reshape(bsz * seq, d)
    for l in range(depth):
        x2d = _layer(
            x2d,
            w_in[l].astype(_BF16),
            b_gate[l].reshape(1, -1),
            ln_v_g[l].reshape(1, -1), ln_v_b[l].reshape(1, -1),
            w_s[l], b_s[l].reshape(N_HEADS, CHUNK, 1),
            conv_w[l], conv_b[l].reshape(1, -1),
            w_oa[l].astype(_BF16), w_ob[l].astype(_BF16), w_out[l].astype(_BF16),
            ln_g[l].reshape(1, -1), ln_b[l].reshape(1, -1),
            seq_len=seq)
    return x2d.reshape(bsz, seq, d)
```

```python
import functools

import jax
import jax.numpy as jnp
from jax import lax
from jax.experimental import pallas as pl
from jax.experimental.pallas import tpu as pltpu

D_MODEL = 2048
CHUNK = 128
HEAD_DIM = 256
N_HEADS = D_MODEL // HEAD_DIM
N_GROUPS = 9
CONV_WIDTH = 3
DN_ALPHA = 2.0 ** 0.25
LN_EPS = 1e-5

TOKEN_TILE = 512
VMEM_LIMIT_BYTES = 56 * 1024 * 1024

_F32 = jnp.float32
_BF16 = jnp.bfloat16


def _dot(a, b):
    return jnp.dot(a, b, preferred_element_type=_F32)


def _fused_kernel(x_ref, wu_ref, wv_ref, wza_ref, wxb_ref, wcb_ref, wbb_ref, wzb_ref,
                  lnvg_ref, lnvb_ref, ws_ref, bs_ref, cw_ref, cbias_ref,
                  woa_ref, wob_ref, wga_ref, wgb_ref, bga_ref, bgb_ref, wout_ref,
                  lng_ref, lnb_ref,
                  o_ref,
                  xbf_ref, ya_ref, yb_ref, hcarry_ref,
                  *, tiles_per_seq):
    i = pl.program_id(0)
    s = pl.program_id(1)
    tm = x_ref.shape[0]

    @pl.when(s == 0)
    def _start_tile():
        x = x_ref[...]
        xbf_ref[...] = x.astype(_BF16)
        o_ref[...] = DN_ALPHA * x

    @pl.when(s < N_HEADS)
    def _phase1():
        xb16 = xbf_ref[...]
        v = jax.nn.gelu(_dot(xb16, wv_ref[...]))
        u = jax.nn.gelu(_dot(xb16, wu_ref[...]))
        za = jax.nn.silu(_dot(xb16, wza_ref[...]))
        h = _dot(xb16, wcb_ref[...]) * _dot(xb16, wxb_ref[...])

        mu = jnp.mean(v, axis=-1, keepdims=True)
        vc = v - mu
        var = jnp.mean(vc * vc, axis=-1, keepdims=True)
        vn = vc * lax.rsqrt(var + LN_EPS) * lnvg_ref[...] + lnvb_ref[...]
        vn16 = vn.astype(_BF16)
        row = lax.broadcasted_iota(jnp.int32, (CHUNK, CHUNK), 0)
        col = lax.broadcasted_iota(jnp.int32, (CHUNK, CHUNK), 1)
        w_mix = jnp.where(row >= col, ws_ref[0], 0.0).astype(_BF16)
        bias_col = bs_ref[0]
        mixed = jnp.concatenate(
            [_dot(w_mix, vn16[c * CHUNK:(c + 1) * CHUNK, :]) + bias_col
             for c in range(tm // CHUNK)], axis=0)
        ya_ref[s] = (u * mixed * za).astype(_BF16)

        @pl.when((i % tiles_per_seq) == 0)
        def _reset_history():
            hcarry_ref[s] = jnp.zeros((8, HEAD_DIM), _F32)

        prev = hcarry_ref[s]
        hcarry_ref[s] = h[tm - 8:, :]
        rid = lax.broadcasted_iota(jnp.int32, h.shape, 0)
        h1 = jnp.where(rid == 0, prev[7:8, :], pltpu.roll(h, 1, 0))
        h2 = jnp.where(rid == 0, prev[6:7, :],
                       jnp.where(rid == 1, prev[7:8, :], pltpu.roll(h, 2, 0)))
        conv = cbias_ref[...] + cw_ref[0:1, :] * h2 + cw_ref[1:2, :] * h1 + cw_ref[2:3, :] * h
        bb = _dot(xb16, wbb_ref[...])
        zb = jax.nn.silu(_dot(xb16, wzb_ref[...]))
        yb_ref[s] = (bb * conv * zb).astype(_BF16)

    @pl.when(s >= N_HEADS)
    def _phase2():
        xb16 = xbf_ref[...]
        a = _dot(ya_ref[0], woa_ref[0:HEAD_DIM, :])
        b = _dot(yb_ref[0], wob_ref[0:HEAD_DIM, :])
        for k in range(1, N_HEADS):
            a += _dot(ya_ref[k], woa_ref[k * HEAD_DIM:(k + 1) * HEAD_DIM, :])
            b += _dot(yb_ref[k], wob_ref[k * HEAD_DIM:(k + 1) * HEAD_DIM, :])
        ga = jax.nn.sigmoid(_dot(xb16, wga_ref[...]) + bga_ref[...])
        gb = jax.nn.sigmoid(_dot(xb16, wgb_ref[...]) + bgb_ref[...])
        merged = (ga * a + gb * b).astype(_BF16)
        o_ref[...] += _dot(merged, wout_ref[...])

        @pl.when(s == 2 * N_HEADS - 1)
        def _post_norm():
            y = o_ref[...]
            mu = jnp.mean(y, axis=-1, keepdims=True)
            yc = y - mu
            var = jnp.mean(yc * yc, axis=-1, keepdims=True)
            o_ref[...] = yc * lax.rsqrt(var + LN_EPS) * lng_ref[...] + lnb_ref[...]


def _layer(x2d, w_in_bf, b_gate, ln_v_g, ln_v_b, w_s, b_s, conv_w, conv_b,
           w_oa_bf, w_ob_bf, w_out_bf, ln_g, ln_b, *, seq_len):
    m = x2d.shape[0]
    tm = TOKEN_TILE
    assert m % tm == 0 and seq_len % tm == 0 and tm % CHUNK == 0
    n_tiles = m // tm
    last = N_HEADS - 1

    def head_of(s):
        return jnp.minimum(s, last)

    def col_of(s):
        return jnp.maximum(s - N_HEADS, 0)

    def w_in_head(g):
        return pl.BlockSpec((D_MODEL, HEAD_DIM), lambda i, s, g=g: (0, g * N_HEADS + head_of(s)))

    def w_in_col(g):
        return pl.BlockSpec((D_MODEL, HEAD_DIM), lambda i, s, g=g: (0, g * N_HEADS + col_of(s)))

    head_vec = pl.BlockSpec((1, HEAD_DIM), lambda i, s: (0, head_of(s)))
    in_specs = [
        pl.BlockSpec((tm, D_MODEL), lambda i, s: (i, 0)),
        w_in_head(0), w_in_head(1), w_in_head(2),
        w_in_head(3), w_in_head(4), w_in_head(5), w_in_head(6),
        head_vec, head_vec,
        pl.BlockSpec((1, CHUNK, CHUNK), lambda i, s: (head_of(s), 0, 0)),
        pl.BlockSpec((1, CHUNK, 1), lambda i, s: (head_of(s), 0, 0)),
        pl.BlockSpec((CONV_WIDTH, HEAD_DIM), lambda i, s: (0, head_of(s))),
        head_vec,
        pl.BlockSpec((D_MODEL, HEAD_DIM), lambda i, s: (0, col_of(s))),
        pl.BlockSpec((D_MODEL, HEAD_DIM), lambda i, s: (0, col_of(s))),
        w_in_col(7), w_in_col(8),
        pl.BlockSpec((1, HEAD_DIM), lambda i, s: (0, col_of(s))),
        pl.BlockSpec((1, HEAD_DIM), lambda i, s: (0, N_HEADS + col_of(s))),
        pl.BlockSpec((HEAD_DIM, D_MODEL), lambda i, s: (col_of(s), 0)),
        pl.BlockSpec((1, D_MODEL), lambda i, s: (0, 0)),
        pl.BlockSpec((1, D_MODEL), lambda i, s: (0, 0)),
    ]
    kern = functools.partial(_fused_kernel, tiles_per_seq=seq_len // tm)
    return pl.pallas_call(
        kern,
        grid=(n_tiles, 2 * N_HEADS),
        in_specs=in_specs,
        out_specs=pl.BlockSpec((tm, D_MODEL), lambda i, s: (i, 0)),
        out_shape=jax.ShapeDtypeStruct((m, D_MODEL), _F32),
        scratch_shapes=[
            pltpu.VMEM((tm, D_MODEL), _BF16),
            pltpu.VMEM((N_HEADS, tm, HEAD_DIM), _BF16),
            pltpu.VMEM((N_HEADS, tm, HEAD_DIM), _BF16),
            pltpu.VMEM((N_HEADS, 8, HEAD_DIM), _F32),
        ],
        compiler_params=pltpu.CompilerParams(
            dimension_semantics=("arbitrary", "arbitrary"),
            vmem_limit_bytes=VMEM_LIMIT_BYTES),
        name="fused_layer",
    )(x2d,
      w_in_bf, w_in_bf, w_in_bf, w_in_bf, w_in_bf, w_in_bf, w_in_bf,
      ln_v_g, ln_v_b, w_s, b_s, conv_w, conv_b,
      w_oa_bf, w_ob_bf, w_in_bf, w_in_bf, b_gate, b_gate, w_out_bf,
      ln_g, ln_b)


def kernel(x, w_in, b_gate, ln_v_g, ln_v_b, w_s, b_s, conv_w, conv_b, w_oa, w_ob, w_out, ln_g, ln_b):
    bsz, seq, d = x.shape
    depth = w_in.shape[0]
    x2d = x.reshape(bsz * seq, d)
    for l in range(depth):
        x2d = _layer(
            x2d,
            w_in[l].astype(_BF16),
            b_gate[l].reshape(1, -1),
            ln_v_g[l].reshape(1, -1), ln_v_b[l].reshape(1, -1),
            w_s[l], b_s[l].reshape(N_HEADS, CHUNK, 1),
            conv_w[l], conv_b[l].reshape(1, -1),
            w_oa[l].astype(_BF16), w_ob[l].astype(_BF16), w_out[l].astype(_BF16),
            ln_g[l].reshape(1, -1), ln_b[l].reshape(1, -1),
            seq_len=seq)
    return x2d.reshape(bsz, seq, d)
```

```python
import functools

import jax
import jax.numpy as jnp
from jax import lax
from jax.experimental import pallas as pl
from jax.experimental.pallas import tpu as pltpu

D_MODEL = 2048
CHUNK = 128
HEAD_DIM = 256
N_HEADS = D_MODEL // HEAD_DIM
N_GROUPS = 9
CONV_WIDTH = 3
DN_ALPHA = 2.0 ** 0.25
LN_EPS = 1e-5

TOKEN_TILE = 512
VMEM_LIMIT_BYTES = 56 * 1024 * 1024

_F32 = jnp.float32
_BF16 = jnp.bfloat16


def _dot(a, b):
    return jnp.dot(a, b, preferred_element_type=_F32)


def _fused_kernel(x_ref, wu_ref, wv_ref, wza_ref, wxb_ref, wcb_ref, wbb_ref, wzb_ref,
                  lnvg_ref, lnvb_ref, ws_ref, bs_ref, cw_ref, cbias_ref,
                  woa_ref, wob_ref, wga_ref, wgb_ref, bga_ref, bgb_ref, wout_ref,
                  lng_ref, lnb_ref,
                  o_ref,
                  xbf_ref, ya_ref, yb_ref, hcarry_ref,
                  *, tiles_per_seq):
    i = pl.program_id(0)
    s = pl.program_id(1)
    tm = x_ref.shape[0]

    @pl.when(s == 0)
    def _start_tile():
        x = x_ref[...]
        xbf_ref[...] = x.astype(_BF16)
        o_ref[...] = DN_ALPHA * x

    @pl.when(s < N_HEADS)
    def _phase1():
        xb16 = xbf_ref[...]
        v = jax.nn.gelu(_dot(xb16, wv_ref[...]))
        u = jax.nn.gelu(_dot(xb16, wu_ref[...]))
        za = jax.nn.silu(_dot(xb16, wza_ref[...]))
        h = _dot(xb16, wcb_ref[...]) * _dot(xb16, wxb_ref[...])

        mu = jnp.mean(v, axis=-1, keepdims=True)
        vc = v - mu
        var = jnp.mean(vc * vc, axis=-1, keepdims=True)
        vn = vc * lax.rsqrt(var + LN_EPS) * lnvg_ref[...] + lnvb_ref[...]
        vn16 = vn.astype(_BF16)
        row = lax.broadcasted_iota(jnp.int32, (CHUNK, CHUNK), 0)
        col = lax.broadcasted_iota(jnp.int32, (CHUNK, CHUNK), 1)
        w_mix = jnp.where(row >= col, ws_ref[0], 0.0).astype(_BF16)
        bias_col = bs_ref[0]
        mixed = jnp.concatenate(
            [_dot(w_mix, vn16[c * CHUNK:(c + 1) * CHUNK, :]) + bias_col
             for c in range(tm // CHUNK)], axis=0)
        ya_ref[s] = (u * mixed * za).astype(_BF16)

        @pl.when((i % tiles_per_seq) == 0)
        def _reset_history():
            hcarry_ref[s] = jnp.zeros((8, HEAD_DIM), _F32)

        prev = hcarry_ref[s]
        hcarry_ref[s] = h[tm - 8:, :]
        rid = lax.broadcasted_iota(jnp.int32, h.shape, 0)
        h1 = jnp.where(rid == 0, prev[7:8, :], pltpu.roll(h, 1, 0))
        h2 = jnp.where(rid == 0, prev[6:7, :],
                       jnp.where(rid == 1, prev[7:8, :], pltpu.roll(h, 2, 0)))
        conv = cbias_ref[...] + cw_ref[0:1, :] * h2 + cw_ref[1:2, :] * h1 + cw_ref[2:3, :] * h
        bb = _dot(xb16, wbb_ref[...])
        zb = jax.nn.silu(_dot(xb16, wzb_ref[...]))
        yb_ref[s] = (bb * conv * zb).astype(_BF16)

    @pl.when(s >= N_HEADS)
    def _phase2():
        xb16 = xbf_ref[...]
        a = _dot(ya_ref[0], woa_ref[0:HEAD_DIM, :])
        b = _dot(yb_ref[0], wob_ref[0:HEAD_DIM, :])
        for k in range(1, N_HEADS):
            a += _dot(ya_ref[k], woa_ref[k * HEAD_DIM:(k + 1) * HEAD_DIM, :])
            b += _dot(yb_ref[k], wob_ref[k * HEAD_DIM:(k + 1) * HEAD_DIM, :])
        ga = jax.nn.sigmoid(_dot(xb16, wga_ref[...]) + bga_ref[...])
        gb = jax.nn.sigmoid(_dot(xb16, wgb_ref[...]) + bgb_ref[...])
        merged = (ga * a + gb * b).astype(_BF16)
        o_ref[...] += _dot(merged, wout_ref[...])

        @pl.when(s == 2 * N_HEADS - 1)
        def _post_norm():
            y = o_ref[...]
            mu = jnp.mean(y, axis=-1, keepdims=True)
            yc = y - mu
            var = jnp.mean(yc * yc, axis=-1, keepdims=True)
            o_ref[...] = yc * lax.rsqrt(var + LN_EPS) * lng_ref[...] + lnb_ref[...]


def _layer(x2d, w_in_bf, b_gate, ln_v_g, ln_v_b, w_s, b_s, conv_w, conv_b,
           w_oa_bf, w_ob_bf, w_out_bf, ln_g, ln_b, *, seq_len):
    m = x2d.shape[0]
    tm = TOKEN_TILE
    assert m % tm == 0 and seq_len % tm == 0 and tm % CHUNK == 0
    n_tiles = m // tm
    last = N_HEADS - 1

    def head_of(s):
        return jnp.minimum(s, last)

    def col_of(s):
        return jnp.maximum(s - N_HEADS, 0)

    def w_in_head(g):
        return pl.BlockSpec((None, D_MODEL, HEAD_DIM), lambda i, s, g=g: (g * N_HEADS + head_of(s), 0, 0))

    def w_in_col(g):
        return pl.BlockSpec((None, D_MODEL, HEAD_DIM), lambda i, s, g=g: (g * N_HEADS + col_of(s), 0, 0))

    head_vec = pl.BlockSpec((1, HEAD_DIM), lambda i, s: (0, head_of(s)))
    in_specs = [
        pl.BlockSpec((tm, D_MODEL), lambda i, s: (i, 0)),
        w_in_head(0), w_in_head(1), w_in_head(2),
        w_in_head(3), w_in_head(4), w_in_head(5), w_in_head(6),
        head_vec, head_vec,
        pl.BlockSpec((1, CHUNK, CHUNK), lambda i, s: (head_of(s), 0, 0)),
        pl.BlockSpec((1, CHUNK, 1), lambda i, s: (head_of(s), 0, 0)),
        pl.BlockSpec((CONV_WIDTH, HEAD_DIM), lambda i, s: (0, head_of(s))),
        head_vec,
        pl.BlockSpec((None, D_MODEL, HEAD_DIM), lambda i, s: (col_of(s), 0, 0)),
        pl.BlockSpec((None, D_MODEL, HEAD_DIM), lambda i, s: (col_of(s), 0, 0)),
        w_in_col(7), w_in_col(8),
        pl.BlockSpec((1, HEAD_DIM), lambda i, s: (0, col_of(s))),
        pl.BlockSpec((1, HEAD_DIM), lambda i, s: (0, N_HEADS + col_of(s))),
        pl.BlockSpec((HEAD_DIM, D_MODEL), lambda i, s: (col_of(s), 0)),
        pl.BlockSpec((1, D_MODEL), lambda i, s: (0, 0)),
        pl.BlockSpec((1, D_MODEL), lambda i, s: (0, 0)),
    ]
    kern = functools.partial(_fused_kernel, tiles_per_seq=seq_len // tm)
    return pl.pallas_call(
        kern,
        grid=(n_tiles, 2 * N_HEADS),
        in_specs=in_specs,
        out_specs=pl.BlockSpec((tm, D_MODEL), lambda i, s: (i, 0)),
        out_shape=jax.ShapeDtypeStruct((m, D_MODEL), _F32),
        scratch_shapes=[
            pltpu.VMEM((tm, D_MODEL), _BF16),
            pltpu.VMEM((N_HEADS, tm, HEAD_DIM), _BF16),
            pltpu.VMEM((N_HEADS, tm, HEAD_DIM), _BF16),
            pltpu.VMEM((N_HEADS, 8, HEAD_DIM), _F32),
        ],
        compiler_params=pltpu.CompilerParams(
            dimension_semantics=("arbitrary", "arbitrary"),
            vmem_limit_bytes=VMEM_LIMIT_BYTES),
        name="fused_layer",
    )(x2d,
      w_in_bf, w_in_bf, w_in_bf, w_in_bf, w_in_bf, w_in_bf, w_in_bf,
      ln_v_g, ln_v_b, w_s, b_s, conv_w, conv_b,
      w_oa_bf, w_ob_bf, w_in_bf, w_in_bf, b_gate, b_gate, w_out_bf,
      ln_g, ln_b)


def _col_blocks(w):
    k, n = w.shape
    return w.astype(_BF16).reshape(k, n // HEAD_DIM, HEAD_DIM).transpose(1, 0, 2)


def kernel(x, w_in, b_gate, ln_v_g, ln_v_b, w_s, b_s, conv_w, conv_b, w_oa, w_ob, w_out, ln_g, ln_b):
    bsz, seq, d = x.shape
    depth = w_in.shape[0]
    x2d = x.reshape(bsz * seq, d)
    for l in range(depth):
        x2d = _layer(
            x2d,
            _col_blocks(w_in[l]),
            b_gate[l].reshape(1, -1),
            ln_v_g[l].reshape(1, -1), ln_v_b[l].reshape(1, -1),
            w_s[l], b_s[l].reshape(N_HEADS, CHUNK, 1),
            conv_w[l], conv_b[l].reshape(1, -1),
            _col_blocks(w_oa[l]), _col_blocks(w_ob[l]), w_out[l].astype(_BF16),
            ln_g[l].reshape(1, -1), ln_b[l].reshape(1, -1),
            seq_len=seq)
    return x2d.reshape(bsz, seq, d)
```

```python
import functools

import jax
import jax.numpy as jnp
from jax import lax
from jax.experimental import pallas as pl
from jax.experimental.pallas import tpu as pltpu

D_MODEL = 2048
CHUNK = 128
HEAD_DIM = 256
N_HEADS = D_MODEL // HEAD_DIM
CONV_WIDTH = 3
DN_ALPHA = 2.0 ** 0.25
LN_EPS = 1e-5

TOKEN_TILE = 1024
ROW_HALF = 512
VMEM_LIMIT_BYTES = 56 * 1024 * 1024

_G_U, _G_V, _G_ZA, _G_XB, _G_CB, _G_BB, _G_ZB, _G_GA, _G_GB = (g * N_HEADS for g in range(9))
_OFF_WOA = 9 * N_HEADS
_OFF_WOB = 10 * N_HEADS
_SLOT_P1 = (_G_V, _G_U, _G_ZA, _G_CB, _G_XB, _G_BB, _G_ZB)
_SLOT_P2 = (_OFF_WOA, _OFF_WOB, _G_GA, _G_GB)
_HP_LNG, _HP_LNB, _HP_CB, _HP_CW0 = 0, 1, 2, 3

_F32 = jnp.float32
_BF16 = jnp.bfloat16


def _dot(a, b):
    return jnp.dot(a, b, preferred_element_type=_F32)


def _fused_kernel(x_hbm, w0, w1, w2, w3, w4, w5, w6, wout_ref, hp_ref, ws_ref, bs_ref, bg_ref, ln_ref,
                  o_hbm,
                  xbuf, acc, xbf_ref, ya_ref, yb_ref, hcarry_ref, sem,
                  *, tiles_per_seq, n_tiles):
    i = pl.program_id(0)
    s = pl.program_id(1)
    tm = xbuf.shape[0]
    n_half = tm // ROW_HALF

    def x_copy(tile):
        return pltpu.make_async_copy(x_hbm.at[pl.ds(tile * tm, tm), :], xbuf, sem.at[0])

    def out_copy(tile):
        return pltpu.make_async_copy(acc, o_hbm.at[pl.ds(tile * tm, tm), :], sem.at[1])

    def phase1(first):
        wv, wu, wza, wcb, wxb, wbb, wzb = w0, w1, w2, w3, w4, w5, w6
        row = lax.broadcasted_iota(jnp.int32, (CHUNK, CHUNK), 0)
        col = lax.broadcasted_iota(jnp.int32, (CHUNK, CHUNK), 1)
        w_mix = jnp.where(row >= col, ws_ref[0], 0.0).astype(_BF16)
        bias_col = bs_ref[0]

        @pl.when((i % tiles_per_seq) == 0)
        def _reset_history():
            hcarry_ref[s] = jnp.zeros((8, HEAD_DIM), _F32)

        prev = hcarry_ref[s]
        for r in range(n_half):
            rows = slice(r * ROW_HALF, (r + 1) * ROW_HALF)
            if first:
                xbf_ref[rows, :] = xbuf[rows, :].astype(_BF16)
            xb16 = xbf_ref[rows, :]
            v = jax.nn.gelu(_dot(xb16, wv[...]))
            u = jax.nn.gelu(_dot(xb16, wu[...]))
            za = jax.nn.silu(_dot(xb16, wza[...]))
            h = _dot(xb16, wcb[...]) * _dot(xb16, wxb[...])

            mu = jnp.mean(v, axis=-1, keepdims=True)
            vc = v - mu
            var = jnp.mean(vc * vc, axis=-1, keepdims=True)
            vn = (vc * lax.rsqrt(var + LN_EPS) * hp_ref[_HP_LNG:_HP_LNG + 1, :]
                  + hp_ref[_HP_LNB:_HP_LNB + 1, :])
            vn16 = vn.astype(_BF16)
            mixed = jnp.concatenate(
                [_dot(w_mix, vn16[c * CHUNK:(c + 1) * CHUNK, :]) + bias_col
                 for c in range(ROW_HALF // CHUNK)], axis=0)
            ya_ref[s, rows, :] = (u * mixed * za).astype(_BF16)

            rid = lax.broadcasted_iota(jnp.int32, h.shape, 0)
            h1 = jnp.where(rid == 0, prev[7:8, :], pltpu.roll(h, 1, 0))
            h2 = jnp.where(rid == 0, prev[6:7, :],
                           jnp.where(rid == 1, prev[7:8, :], pltpu.roll(h, 2, 0)))
            conv = (hp_ref[_HP_CB:_HP_CB + 1, :]
                    + hp_ref[_HP_CW0:_HP_CW0 + 1, :] * h2
                    + hp_ref[_HP_CW0 + 1:_HP_CW0 + 2, :] * h1
                    + hp_ref[_HP_CW0 + 2:_HP_CW0 + 3, :] * h)
            prev = h[ROW_HALF - 8:, :]
            bb = _dot(xb16, wbb[...])
            zb = jax.nn.silu(_dot(xb16, wzb[...]))
            yb_ref[s, rows, :] = (bb * conv * zb).astype(_BF16)
        hcarry_ref[s] = prev

    def phase2(seed, last):
        woa, wob, wga, wgb = w0, w1, w2, w3
        for r in range(n_half):
            rows = slice(r * ROW_HALF, (r + 1) * ROW_HALF)
            if seed:
                acc[rows, :] = DN_ALPHA * xbuf[rows, :]
            xb16 = xbf_ref[rows, :]
            a = _dot(ya_ref[0, rows, :], woa[0:HEAD_DIM, :])
            b = _dot(yb_ref[0, rows, :], wob[0:HEAD_DIM, :])
            for k in range(1, N_HEADS):
                a += _dot(ya_ref[k, rows, :], woa[k * HEAD_DIM:(k + 1) * HEAD_DIM, :])
                b += _dot(yb_ref[k, rows, :], wob[k * HEAD_DIM:(k + 1) * HEAD_DIM, :])
            ga = jax.nn.sigmoid(_dot(xb16, wga[...]) + bg_ref[0:1, :])
            gb = jax.nn.sigmoid(_dot(xb16, wgb[...]) + bg_ref[1:2, :])
            merged = (ga * a + gb * b).astype(_BF16)
            acc[rows, :] += _dot(merged, wout_ref[...])
            if last:
                y = acc[rows, :]
                mu = jnp.mean(y, axis=-1, keepdims=True)
                yc = y - mu
                var = jnp.mean(yc * yc, axis=-1, keepdims=True)
                acc[rows, :] = yc * lax.rsqrt(var + LN_EPS) * ln_ref[0:1, :] + ln_ref[1:2, :]

    @pl.when(s == 0)
    def _first_head():
        @pl.when(i == 0)
        def _():
            x_copy(i).start()
        x_copy(i).wait()
        phase1(True)

    @pl.when(jnp.logical_and(s > 0, s < N_HEADS))
    def _other_heads():
        phase1(False)

    @pl.when(s == N_HEADS)
    def _first_col():
        @pl.when(i > 0)
        def _():
            out_copy(i - 1).wait()
        phase2(True, False)

        @pl.when(i + 1 < n_tiles)
        def _():
            x_copy(i + 1).start()

    @pl.when(jnp.logical_and(s > N_HEADS, s < 2 * N_HEADS - 1))
    def _other_cols():
        phase2(False, False)

    @pl.when(s == 2 * N_HEADS - 1)
    def _last_col():
        phase2(False, True)
        out_copy(i).start()

        @pl.when(i == n_tiles - 1)
        def _():
            out_copy(i).wait()


def _layer(x2d, w_cat, w_out_bf, head_params, w_s, b_s_col, gate_bias, ln_params, *, seq_len):
    m = x2d.shape[0]
    tm = TOKEN_TILE
    assert m % tm == 0 and seq_len % tm == 0 and tm % ROW_HALF == 0 and ROW_HALF % CHUNK == 0
    n_tiles = m // tm
    last = N_HEADS - 1

    def head_of(s):
        return jnp.minimum(s, last)

    def col_of(s):
        return jnp.maximum(s - N_HEADS, 0)

    def w_slot(k):
        if k < len(_SLOT_P2):
            def idx(i, s, k=k):
                return (0, jnp.where(s < N_HEADS, _SLOT_P1[k] + s, _SLOT_P2[k] + col_of(s)))
        else:
            def idx(i, s, k=k):
                return (0, _SLOT_P1[k] + head_of(s))
        return pl.BlockSpec((D_MODEL, HEAD_DIM), idx)

    in_specs = [pl.BlockSpec(memory_space=pl.ANY)]
    in_specs += [w_slot(k) for k in range(len(_SLOT_P1))]
    in_specs += [
        pl.BlockSpec((HEAD_DIM, D_MODEL), lambda i, s: (col_of(s), 0)),
        pl.BlockSpec((8, HEAD_DIM), lambda i, s: (0, head_of(s))),
        pl.BlockSpec((1, CHUNK, CHUNK), lambda i, s: (head_of(s), 0, 0)),
        pl.BlockSpec((1, CHUNK, 1), lambda i, s: (head_of(s), 0, 0)),
        pl.BlockSpec((2, HEAD_DIM), lambda i, s: (0, col_of(s))),
        pl.BlockSpec((2, D_MODEL), lambda i, s: (0, 0)),
    ]
    kern = functools.partial(_fused_kernel, tiles_per_seq=seq_len // tm, n_tiles=n_tiles)
    return pl.pallas_call(
        kern,
        grid=(n_tiles, 2 * N_HEADS),
        in_specs=in_specs,
        out_specs=pl.BlockSpec(memory_space=pl.ANY),
        out_shape=jax.ShapeDtypeStruct((m, D_MODEL), _F32),
        scratch_shapes=[
            pltpu.VMEM((tm, D_MODEL), _F32),
            pltpu.VMEM((tm, D_MODEL), _F32),
            pltpu.VMEM((tm, D_MODEL), _BF16),
            pltpu.VMEM((N_HEADS, tm, HEAD_DIM), _BF16),
            pltpu.VMEM((N_HEADS, tm, HEAD_DIM), _BF16),
            pltpu.VMEM((N_HEADS, 8, HEAD_DIM), _F32),
            pltpu.SemaphoreType.DMA((2,)),
        ],
        compiler_params=pltpu.CompilerParams(
            dimension_semantics=("arbitrary", "arbitrary"),
            vmem_limit_bytes=VMEM_LIMIT_BYTES),
        name="fused_layer",
    )(x2d, *([w_cat] * len(_SLOT_P1)), w_out_bf, head_params, w_s, b_s_col, gate_bias, ln_params)


def kernel(x, w_in, b_gate, ln_v_g, ln_v_b, w_s, b_s, conv_w, conv_b, w_oa, w_ob, w_out, ln_g, ln_b):
    bsz, seq, d = x.shape
    depth = w_in.shape[0]
    x2d = x.reshape(bsz * seq, d)
    for l in range(depth):
        w_cat = jnp.concatenate([w_in[l], w_oa[l], w_ob[l]], axis=1).astype(_BF16)
        head_params = jnp.concatenate(
            [ln_v_g[l][None], ln_v_b[l][None], conv_b[l][None], conv_w[l],
             jnp.zeros((8 - 3 - CONV_WIDTH, d), _F32)], axis=0)
        x2d = _layer(
            x2d, w_cat, w_out[l].astype(_BF16), head_params,
            w_s[l], b_s[l].reshape(N_HEADS, CHUNK, 1),
            b_gate[l].reshape(2, d), jnp.stack([ln_g[l], ln_b[l]]),
            seq_len=seq)
    return x2d.reshape(bsz, seq, d)
```

```python
import functools

import jax
import jax.numpy as jnp
from jax import lax
from jax.experimental import pallas as pl
from jax.experimental.pallas import tpu as pltpu

D_MODEL = 2048
CHUNK = 128
HEAD_DIM = 256
N_HEADS = D_MODEL // HEAD_DIM
CONV_WIDTH = 3
DN_ALPHA = 2.0 ** 0.25
LN_EPS = 1e-5

TOKEN_TILE = 1024
ROW_HALF = 512
P2_ROWS = 512
VMEM_LIMIT_BYTES = 56 * 1024 * 1024

_G_U, _G_V, _G_ZA, _G_XB, _G_CB, _G_BB, _G_ZB, _G_GA, _G_GB = (g * N_HEADS for g in range(9))
_SLOT_P1 = (_G_V, _G_U, _G_ZA, _G_CB, _G_XB, _G_BB, _G_ZB)
_SLOT_P2 = (_G_GA, _G_GB)
_HP_LNG, _HP_LNB, _HP_CB, _HP_CW0 = 0, 1, 2, 3

_F32 = jnp.float32
_BF16 = jnp.bfloat16


def _dot(a, b):
    return jnp.dot(a, b, preferred_element_type=_F32)


def _fused_kernel(x_hbm, w0, w1, w2, w3, w4, w5, w6, woa, wob, wout_ref,
                  hp_ref, ws_ref, bs_ref, bg_ref, ln_ref,
                  o_hbm,
                  xbuf, acc, xbf_ref, ya_ref, yb_ref, hcarry_ref, sem,
                  *, tiles_per_seq, n_tiles):
    i = pl.program_id(0)
    s = pl.program_id(1)
    tm = xbuf.shape[0]
    n_half = tm // ROW_HALF

    def x_copy(tile):
        return pltpu.make_async_copy(x_hbm.at[pl.ds(tile * tm, tm), :], xbuf, sem.at[0])

    def out_copy(tile):
        return pltpu.make_async_copy(acc, o_hbm.at[pl.ds(tile * tm, tm), :], sem.at[1])

    def phase1(first):
        wv, wu, wza, wcb, wxb, wbb, wzb = w0, w1, w2, w3, w4, w5, w6
        row = lax.broadcasted_iota(jnp.int32, (CHUNK, CHUNK), 0)
        col = lax.broadcasted_iota(jnp.int32, (CHUNK, CHUNK), 1)
        w_mix = jnp.where(row >= col, ws_ref[0], 0.0).astype(_BF16)
        bias_col = bs_ref[0]

        @pl.when((i % tiles_per_seq) == 0)
        def _reset_history():
            hcarry_ref[s] = jnp.zeros((8, HEAD_DIM), _F32)

        prev = hcarry_ref[s]
        for r in range(n_half):
            rows = slice(r * ROW_HALF, (r + 1) * ROW_HALF)
            if first:
                xbf_ref[rows, :] = xbuf[rows, :].astype(_BF16)
            xb16 = xbf_ref[rows, :]
            v = jax.nn.gelu(_dot(xb16, wv[...]))
            u = jax.nn.gelu(_dot(xb16, wu[...]))
            za = jax.nn.silu(_dot(xb16, wza[...]))
            h = _dot(xb16, wcb[...]) * _dot(xb16, wxb[...])

            mu = jnp.mean(v, axis=-1, keepdims=True)
            vc = v - mu
            var = jnp.mean(vc * vc, axis=-1, keepdims=True)
            vn = (vc * lax.rsqrt(var + LN_EPS) * hp_ref[_HP_LNG:_HP_LNG + 1, :]
                  + hp_ref[_HP_LNB:_HP_LNB + 1, :])
            vn16 = vn.astype(_BF16)
            mixed = jnp.concatenate(
                [_dot(w_mix, vn16[c * CHUNK:(c + 1) * CHUNK, :]) + bias_col
                 for c in range(ROW_HALF // CHUNK)], axis=0)
            ya_ref[s, rows, :] = (u * mixed * za).astype(_BF16)

            rid = lax.broadcasted_iota(jnp.int32, h.shape, 0)
            h1 = jnp.where(rid == 0, prev[7:8, :], pltpu.roll(h, 1, 0))
            h2 = jnp.where(rid == 0, prev[6:7, :],
                           jnp.where(rid == 1, prev[7:8, :], pltpu.roll(h, 2, 0)))
            conv = (hp_ref[_HP_CB:_HP_CB + 1, :]
                    + hp_ref[_HP_CW0:_HP_CW0 + 1, :] * h2
                    + hp_ref[_HP_CW0 + 1:_HP_CW0 + 2, :] * h1
                    + hp_ref[_HP_CW0 + 2:_HP_CW0 + 3, :] * h)
            prev = h[ROW_HALF - 8:, :]
            bb = _dot(xb16, wbb[...])
            zb = jax.nn.silu(_dot(xb16, wzb[...]))
            yb_ref[s, rows, :] = (bb * conv * zb).astype(_BF16)
        hcarry_ref[s] = prev

    def phase2(seed, last):
        wga, wgb = w0, w1
        for r in range(tm // P2_ROWS):
            rows = slice(r * P2_ROWS, (r + 1) * P2_ROWS)
            if seed:
                acc[rows, :] = DN_ALPHA * xbuf[rows, :]
            xb16 = xbf_ref[rows, :]
            a = _dot(ya_ref[0, rows, :], woa[0:HEAD_DIM, :])
            b = _dot(yb_ref[0, rows, :], wob[0:HEAD_DIM, :])
            for k in range(1, N_HEADS):
                a += _dot(ya_ref[k, rows, :], woa[k * HEAD_DIM:(k + 1) * HEAD_DIM, :])
                b += _dot(yb_ref[k, rows, :], wob[k * HEAD_DIM:(k + 1) * HEAD_DIM, :])
            ga = jax.nn.sigmoid(_dot(xb16, wga[...]) + bg_ref[0:1, :])
            gb = jax.nn.sigmoid(_dot(xb16, wgb[...]) + bg_ref[1:2, :])
            merged = (ga * a + gb * b).astype(_BF16)
            acc[rows, :] += _dot(merged, wout_ref[...])
            if last:
                y = acc[rows, :]
                mu = jnp.mean(y, axis=-1, keepdims=True)
                yc = y - mu
                var = jnp.mean(yc * yc, axis=-1, keepdims=True)
                acc[rows, :] = yc * lax.rsqrt(var + LN_EPS) * ln_ref[0:1, :] + ln_ref[1:2, :]

    @pl.when(s == 0)
    def _first_head():
        @pl.when(i == 0)
        def _():
            x_copy(i).start()
        x_copy(i).wait()
        phase1(True)

    @pl.when(jnp.logical_and(s > 0, s < N_HEADS))
    def _other_heads():
        phase1(False)

    @pl.when(s == N_HEADS)
    def _first_col():
        @pl.when(i > 0)
        def _():
            out_copy(i - 1).wait()
        phase2(True, False)

        @pl.when(i + 1 < n_tiles)
        def _():
            x_copy(i + 1).start()

    @pl.when(jnp.logical_and(s > N_HEADS, s < 2 * N_HEADS - 1))
    def _other_cols():
        phase2(False, False)

    @pl.when(s == 2 * N_HEADS - 1)
    def _last_col():
        phase2(False, True)
        out_copy(i).start()

        @pl.when(i == n_tiles - 1)
        def _():
            out_copy(i).wait()


def _layer(x2d, w_in_bf, w_oa_bf, w_ob_bf, w_out_bf, head_params, w_s, b_s_col, gate_bias, ln_params,
           *, seq_len):
    m = x2d.shape[0]
    tm = TOKEN_TILE
    assert m % tm == 0 and seq_len % tm == 0 and tm % ROW_HALF == 0 and ROW_HALF % CHUNK == 0
    n_tiles = m // tm
    last = N_HEADS - 1

    def head_of(s):
        return jnp.minimum(s, last)

    def col_of(s):
        return jnp.maximum(s - N_HEADS, 0)

    def w_slot(k):
        if k < len(_SLOT_P2):
            def idx(i, s, k=k):
                return (0, jnp.where(s < N_HEADS, _SLOT_P1[k] + s, _SLOT_P2[k] + col_of(s)))
        else:
            def idx(i, s, k=k):
                return (0, _SLOT_P1[k] + head_of(s))
        return pl.BlockSpec((D_MODEL, HEAD_DIM), idx)

    in_specs = [pl.BlockSpec(memory_space=pl.ANY)]
    in_specs += [w_slot(k) for k in range(len(_SLOT_P1))]
    in_specs += [
        pl.BlockSpec((D_MODEL, HEAD_DIM), lambda i, s: (0, col_of(s))),
        pl.BlockSpec((D_MODEL, HEAD_DIM), lambda i, s: (0, col_of(s))),
        pl.BlockSpec((HEAD_DIM, D_MODEL), lambda i, s: (col_of(s), 0)),
        pl.BlockSpec((8, HEAD_DIM), lambda i, s: (0, head_of(s))),
        pl.BlockSpec((1, CHUNK, CHUNK), lambda i, s: (head_of(s), 0, 0)),
        pl.BlockSpec((1, CHUNK, 1), lambda i, s: (head_of(s), 0, 0)),
        pl.BlockSpec((2, HEAD_DIM), lambda i, s: (0, col_of(s))),
        pl.BlockSpec((2, D_MODEL), lambda i, s: (0, 0)),
    ]
    kern = functools.partial(_fused_kernel, tiles_per_seq=seq_len // tm, n_tiles=n_tiles)
    return pl.pallas_call(
        kern,
        grid=(n_tiles, 2 * N_HEADS),
        in_specs=in_specs,
        out_specs=pl.BlockSpec(memory_space=pl.ANY),
        out_shape=jax.ShapeDtypeStruct((m, D_MODEL), _F32),
        scratch_shapes=[
            pltpu.VMEM((tm, D_MODEL), _F32),
            pltpu.VMEM((tm, D_MODEL), _F32),
            pltpu.VMEM((tm, D_MODEL), _BF16),
            pltpu.VMEM((N_HEADS, tm, HEAD_DIM), _BF16),
            pltpu.VMEM((N_HEADS, tm, HEAD_DIM), _BF16),
            pltpu.VMEM((N_HEADS, 8, HEAD_DIM), _F32),
            pltpu.SemaphoreType.DMA((2,)),
        ],
        compiler_params=pltpu.CompilerParams(
            dimension_semantics=("arbitrary", "arbitrary"),
            vmem_limit_bytes=VMEM_LIMIT_BYTES),
        name="fused_layer",
    )(x2d, *([w_in_bf] * len(_SLOT_P1)), w_oa_bf, w_ob_bf, w_out_bf,
      head_params, w_s, b_s_col, gate_bias, ln_params)


def kernel(x, w_in, b_gate, ln_v_g, ln_v_b, w_s, b_s, conv_w, conv_b, w_oa, w_ob, w_out, ln_g, ln_b):
    bsz, seq, d = x.shape
    depth = w_in.shape[0]
    x2d = x.reshape(bsz * seq, d)
    for l in range(depth):
        head_params = jnp.concatenate(
            [ln_v_g[l][None], ln_v_b[l][None], conv_b[l][None], conv_w[l],
             jnp.zeros((8 - 3 - CONV_WIDTH, d), _F32)], axis=0)
        x2d = _layer(
            x2d, w_in[l].astype(_BF16), w_oa[l].astype(_BF16), w_ob[l].astype(_BF16),
            w_out[l].astype(_BF16), head_params,
            w_s[l], b_s[l].reshape(N_HEADS, CHUNK, 1),
            b_gate[l].reshape(2, d), jnp.stack([ln_g[l], ln_b[l]]),
            seq_len=seq)
    return x2d.reshape(bsz, seq, d)
```

```python
import functools

import jax
import jax.numpy as jnp
from jax import lax
from jax.experimental import pallas as pl
from jax.experimental.pallas import tpu as pltpu

D_MODEL = 2048
CHUNK = 128
HEAD_DIM = 256
N_HEADS = D_MODEL // HEAD_DIM
CONV_WIDTH = 3
DN_ALPHA = 2.0 ** 0.25
LN_EPS = 1e-5

TOKEN_TILE = 1024
ROW_HALF = 512
P2_ROWS = 512
VMEM_LIMIT_BYTES = 56 * 1024 * 1024

_G_U, _G_V, _G_ZA, _G_XB, _G_CB, _G_BB, _G_ZB, _G_GA, _G_GB = (g * N_HEADS for g in range(9))
_SLOT_P1 = (_G_V, _G_U, _G_ZA, _G_CB, _G_XB, _G_BB, _G_ZB)
_SLOT_P2 = (_G_GA, _G_GB)
_HP_LNG, _HP_LNB, _HP_CB, _HP_CW0 = 0, 1, 2, 3

_F32 = jnp.float32
_BF16 = jnp.bfloat16


def _dot(a, b):
    return jnp.dot(a, b, preferred_element_type=_F32)


def _sigmoid(t):
    return 0.5 * jnp.tanh(0.5 * t) + 0.5


def _silu(t):
    return t * _sigmoid(t)


def _fused_kernel(x_hbm, w0, w1, w2, w3, w4, w5, w6, woa, wob, wout_ref,
                  hp_ref, ws_ref, bs_ref, bg_ref, ln_ref,
                  o_hbm,
                  xbuf, acc, xbf_ref, ya_ref, yb_ref, hcarry_ref, sem,
                  *, tiles_per_seq, n_tiles):
    i = pl.program_id(0)
    s = pl.program_id(1)
    tm = xbuf.shape[0]
    n_half = tm // ROW_HALF

    def x_copy(tile):
        return pltpu.make_async_copy(x_hbm.at[pl.ds(tile * tm, tm), :], xbuf, sem.at[0])

    def out_copy(tile):
        return pltpu.make_async_copy(acc, o_hbm.at[pl.ds(tile * tm, tm), :], sem.at[1])

    def phase1(first):
        wv, wu, wza, wcb, wxb, wbb, wzb = w0, w1, w2, w3, w4, w5, w6
        row = lax.broadcasted_iota(jnp.int32, (CHUNK, CHUNK), 0)
        col = lax.broadcasted_iota(jnp.int32, (CHUNK, CHUNK), 1)
        w_mix = jnp.where(row >= col, ws_ref[s], 0.0).astype(_BF16)
        bias_col = bs_ref[s]
        hp = hp_ref[s]

        @pl.when((i % tiles_per_seq) == 0)
        def _reset_history():
            hcarry_ref[s] = jnp.zeros((8, HEAD_DIM), _F32)

        prev = hcarry_ref[s]
        for r in range(n_half):
            rows = slice(r * ROW_HALF, (r + 1) * ROW_HALF)
            if first:
                xbf_ref[rows, :] = xbuf[rows, :].astype(_BF16)
            xb16 = xbf_ref[rows, :]
            v = jax.nn.gelu(_dot(xb16, wv[...]))
            u = jax.nn.gelu(_dot(xb16, wu[...]))
            za = _silu(_dot(xb16, wza[...]))
            h = _dot(xb16, wcb[...]) * _dot(xb16, wxb[...])

            mu = jnp.mean(v, axis=-1, keepdims=True)
            vc = v - mu
            var = jnp.mean(vc * vc, axis=-1, keepdims=True)
            vn = (vc * lax.rsqrt(var + LN_EPS) * hp[_HP_LNG:_HP_LNG + 1, :]
                  + hp[_HP_LNB:_HP_LNB + 1, :])
            vn16 = vn.astype(_BF16)
            mixed = jnp.concatenate(
                [_dot(w_mix, vn16[c * CHUNK:(c + 1) * CHUNK, :]) + bias_col
                 for c in range(ROW_HALF // CHUNK)], axis=0)
            ya_ref[s, rows, :] = (u * mixed * za).astype(_BF16)

            rid = lax.broadcasted_iota(jnp.int32, h.shape, 0)
            h1 = jnp.where(rid == 0, prev[7:8, :], pltpu.roll(h, 1, 0))
            h2 = jnp.where(rid == 0, prev[6:7, :],
                           jnp.where(rid == 1, prev[7:8, :], pltpu.roll(h, 2, 0)))
            conv = (hp[_HP_CB:_HP_CB + 1, :]
                    + hp[_HP_CW0:_HP_CW0 + 1, :] * h2
                    + hp[_HP_CW0 + 1:_HP_CW0 + 2, :] * h1
                    + hp[_HP_CW0 + 2:_HP_CW0 + 3, :] * h)
            prev = h[ROW_HALF - 8:, :]
            zb = _silu(_dot(xb16, wzb[...]))
            bb = _dot(xb16, wbb[...])
            yb_ref[s, rows, :] = (bb * (conv * zb)).astype(_BF16)
        hcarry_ref[s] = prev

    def phase2(seed, last):
        wga, wgb = w0, w1
        n_pass = tm // P2_ROWS
        gate_bias = bg_ref[s - N_HEADS]

        def merge_pass(r):
            rows = slice(r * P2_ROWS, (r + 1) * P2_ROWS)
            if seed:
                acc[rows, :] = DN_ALPHA * xbuf[rows, :]
            xb16 = xbf_ref[rows, :]
            a = _dot(ya_ref[0, rows, :], woa[0:HEAD_DIM, :])
            b = _dot(yb_ref[0, rows, :], wob[0:HEAD_DIM, :])
            for k in range(1, N_HEADS):
                a += _dot(ya_ref[k, rows, :], woa[k * HEAD_DIM:(k + 1) * HEAD_DIM, :])
                b += _dot(yb_ref[k, rows, :], wob[k * HEAD_DIM:(k + 1) * HEAD_DIM, :])
            ga = _sigmoid(_dot(xb16, wga[...]) + gate_bias[0:1, :])
            gb = _sigmoid(_dot(xb16, wgb[...]) + gate_bias[1:2, :])
            return (ga * a + gb * b).astype(_BF16)

        def out_pass(r, merged):
            rows = slice(r * P2_ROWS, (r + 1) * P2_ROWS)
            acc[rows, :] += _dot(merged, wout_ref[...])
            if last:
                y = acc[rows, :]
                mu = jnp.mean(y, axis=-1, keepdims=True)
                yc = y - mu
                var = jnp.mean(yc * yc, axis=-1, keepdims=True)
                acc[rows, :] = yc * lax.rsqrt(var + LN_EPS) * ln_ref[0:1, :] + ln_ref[1:2, :]

        if last:
            for r in range(n_pass):
                out_pass(r, merge_pass(r))
        else:
            merged = [merge_pass(r) for r in range(n_pass)]
            for r in range(n_pass):
                out_pass(r, merged[r])

    @pl.when(s == 0)
    def _first_head():
        @pl.when(i == 0)
        def _():
            x_copy(i).start()
        x_copy(i).wait()
        phase1(True)

    @pl.when(jnp.logical_and(s > 0, s < N_HEADS))
    def _other_heads():
        phase1(False)

    @pl.when(s == N_HEADS)
    def _first_col():
        @pl.when(i > 0)
        def _():
            out_copy(i - 1).wait()
        phase2(True, False)

        @pl.when(i + 1 < n_tiles)
        def _():
            x_copy(i + 1).start()

    @pl.when(jnp.logical_and(s > N_HEADS, s < 2 * N_HEADS - 1))
    def _other_cols():
        phase2(False, False)

    @pl.when(s == 2 * N_HEADS - 1)
    def _last_col():
        phase2(False, True)
        out_copy(i).start()

        @pl.when(i == n_tiles - 1)
        def _():
            out_copy(i).wait()


def _layer(x2d, w_in_bf, w_oa_bf, w_ob_bf, w_out_bf, head_params, w_s, b_s_col, gate_bias, ln_params,
           *, seq_len):
    m = x2d.shape[0]
    tm = TOKEN_TILE
    assert m % tm == 0 and seq_len % tm == 0 and tm % ROW_HALF == 0 and ROW_HALF % CHUNK == 0
    n_tiles = m // tm
    last = N_HEADS - 1

    def head_of(s):
        return jnp.minimum(s, last)

    def col_of(s):
        return jnp.maximum(s - N_HEADS, 0)

    def w_slot(k):
        if k < len(_SLOT_P2):
            def idx(i, s, k=k):
                return (0, jnp.where(s < N_HEADS, _SLOT_P1[k] + s, _SLOT_P2[k] + col_of(s)))
        else:
            def idx(i, s, k=k):
                return (0, _SLOT_P1[k] + head_of(s))
        return pl.BlockSpec((D_MODEL, HEAD_DIM), idx)

    in_specs = [pl.BlockSpec(memory_space=pl.ANY)]
    in_specs += [w_slot(k) for k in range(len(_SLOT_P1))]
    in_specs += [
        pl.BlockSpec((D_MODEL, HEAD_DIM), lambda i, s: (0, col_of(s))),
        pl.BlockSpec((D_MODEL, HEAD_DIM), lambda i, s: (0, col_of(s))),
        pl.BlockSpec((HEAD_DIM, D_MODEL), lambda i, s: (col_of(s), 0)),
        pl.BlockSpec((N_HEADS, 8, HEAD_DIM), lambda i, s: (0, 0, 0)),
        pl.BlockSpec((N_HEADS, CHUNK, CHUNK), lambda i, s: (0, 0, 0)),
        pl.BlockSpec((N_HEADS, CHUNK, 1), lambda i, s: (0, 0, 0)),
        pl.BlockSpec((N_HEADS, 2, HEAD_DIM), lambda i, s: (0, 0, 0)),
        pl.BlockSpec((2, D_MODEL), lambda i, s: (0, 0)),
    ]
    kern = functools.partial(_fused_kernel, tiles_per_seq=seq_len // tm, n_tiles=n_tiles)
    return pl.pallas_call(
        kern,
        grid=(n_tiles, 2 * N_HEADS),
        in_specs=in_specs,
        out_specs=pl.BlockSpec(memory_space=pl.ANY),
        out_shape=jax.ShapeDtypeStruct((m, D_MODEL), _F32),
        scratch_shapes=[
            pltpu.VMEM((tm, D_MODEL), _F32),
            pltpu.VMEM((tm, D_MODEL), _F32),
            pltpu.VMEM((tm, D_MODEL), _BF16),
            pltpu.VMEM((N_HEADS, tm, HEAD_DIM), _BF16),
            pltpu.VMEM((N_HEADS, tm, HEAD_DIM), _BF16),
            pltpu.VMEM((N_HEADS, 8, HEAD_DIM), _F32),
            pltpu.SemaphoreType.DMA((2,)),
        ],
        compiler_params=pltpu.CompilerParams(
            dimension_semantics=("arbitrary", "arbitrary"),
            vmem_limit_bytes=VMEM_LIMIT_BYTES),
        name="fused_layer",
    )(x2d, *([w_in_bf] * len(_SLOT_P1)), w_oa_bf, w_ob_bf, w_out_bf,
      head_params, w_s, b_s_col, gate_bias, ln_params)


def kernel(x, w_in, b_gate, ln_v_g, ln_v_b, w_s, b_s, conv_w, conv_b, w_oa, w_ob, w_out, ln_g, ln_b):
    bsz, seq, d = x.shape
    depth = w_in.shape[0]
    x2d = x.reshape(bsz * seq, d)
    for l in range(depth):
        head_params = jnp.concatenate(
            [ln_v_g[l][None], ln_v_b[l][None], conv_b[l][None], conv_w[l],
             jnp.zeros((8 - 3 - CONV_WIDTH, d), _F32)], axis=0)
        head_params = head_params.reshape(8, N_HEADS, HEAD_DIM).transpose(1, 0, 2)
        gate_bias = b_gate[l].reshape(2, N_HEADS, HEAD_DIM).transpose(1, 0, 2)
        x2d = _layer(
            x2d, w_in[l].astype(_BF16), w_oa[l].astype(_BF16), w_ob[l].astype(_BF16),
            w_out[l].astype(_BF16), head_params,
            w_s[l], b_s[l].reshape(N_HEADS, CHUNK, 1),
            gate_bias, jnp.stack([ln_g[l], ln_b[l]]),
            seq_len=seq)
    return x2d.reshape(bsz, seq, d)
```

```python
import functools

import jax
import jax.numpy as jnp
from jax import lax
from jax.experimental import pallas as pl
from jax.experimental.pallas import tpu as pltpu

D_MODEL = 2048
CHUNK = 128
HEAD_DIM = 256
N_HEADS = D_MODEL // HEAD_DIM
CONV_WIDTH = 3
DN_ALPHA = 2.0 ** 0.25
LN_EPS = 1e-5

TOKEN_TILE = 1024
ROW_HALF = 512
P2_ROWS = 512
VMEM_LIMIT_BYTES = 60 * 1024 * 1024

_G_U, _G_V, _G_ZA, _G_XB, _G_CB, _G_BB, _G_ZB, _G_GA, _G_GB = (g * N_HEADS for g in range(9))
_SLOT_P1 = (_G_V, _G_U, _G_ZA, _G_CB, _G_XB, _G_BB, _G_ZB)
_SLOT_P2 = (_G_GA, _G_GB)
_HP_LNG, _HP_LNB, _HP_CB, _HP_CW0 = 0, 1, 2, 3

_F32 = jnp.float32
_BF16 = jnp.bfloat16


def _dot(a, b):
    return jnp.dot(a, b, preferred_element_type=_F32)


def _sigmoid(t):
    return 0.5 * jnp.tanh(0.5 * t) + 0.5


def _silu(t):
    return t * _sigmoid(t)


def _fused_kernel(x_hbm, w0, w1, w2, w3, w4, w5, w6, woa, wob, wout_ref,
                  hp_ref, ws_ref, bs_ref, bg_ref, ln_ref,
                  o_hbm,
                  xbuf, acc, xbf_ref, ya_ref, yb_ref, hcarry_ref, sem,
                  *, tiles_per_seq, n_tiles):
    i = pl.program_id(0)
    s = pl.program_id(1)
    tm = xbuf.shape[0]
    n_half = tm // ROW_HALF

    def x_copy(tile):
        return pltpu.make_async_copy(x_hbm.at[pl.ds(tile * tm, tm), :], xbuf, sem.at[0])

    def out_copy(tile):
        return pltpu.make_async_copy(acc, o_hbm.at[pl.ds(tile * tm, tm), :], sem.at[1])

    def phase1(first):
        wv, wu, wza, wcb, wxb, wbb, wzb = w0, w1, w2, w3, w4, w5, w6
        row = lax.broadcasted_iota(jnp.int32, (CHUNK, CHUNK), 0)
        col = lax.broadcasted_iota(jnp.int32, (CHUNK, CHUNK), 1)
        w_mix = jnp.where(row >= col, ws_ref[s], 0.0).astype(_BF16)
        bias_col = bs_ref[s]
        hp = hp_ref[s]

        @pl.when((i % tiles_per_seq) == 0)
        def _reset_history():
            hcarry_ref[s] = jnp.zeros((8, HEAD_DIM), _F32)

        prev = hcarry_ref[s]
        for r in range(n_half):
            rows = slice(r * ROW_HALF, (r + 1) * ROW_HALF)
            if first:
                xbf_ref[rows, :] = xbuf[rows, :].astype(_BF16)
            xb16 = xbf_ref[rows, :]
            v = jax.nn.gelu(_dot(xb16, wv[...]))
            u = jax.nn.gelu(_dot(xb16, wu[...]))
            za = _silu(_dot(xb16, wza[...]))
            h = _dot(xb16, wcb[...]) * _dot(xb16, wxb[...])

            mu = jnp.mean(v, axis=-1, keepdims=True)
            vc = v - mu
            var = jnp.mean(vc * vc, axis=-1, keepdims=True)
            vn = (vc * lax.rsqrt(var + LN_EPS) * hp[_HP_LNG:_HP_LNG + 1, :]
                  + hp[_HP_LNB:_HP_LNB + 1, :])
            vn16 = vn.astype(_BF16)
            mixed = jnp.concatenate(
                [_dot(w_mix, vn16[c * CHUNK:(c + 1) * CHUNK, :]) + bias_col
                 for c in range(ROW_HALF // CHUNK)], axis=0)
            ya_ref[s, rows, :] = (u * mixed * za).astype(_BF16)

            rid = lax.broadcasted_iota(jnp.int32, h.shape, 0)
            h1 = jnp.where(rid == 0, prev[7:8, :], pltpu.roll(h, 1, 0))
            h2 = jnp.where(rid == 0, prev[6:7, :],
                           jnp.where(rid == 1, prev[7:8, :], pltpu.roll(h, 2, 0)))
            conv = (hp[_HP_CB:_HP_CB + 1, :]
                    + hp[_HP_CW0:_HP_CW0 + 1, :] * h2
                    + hp[_HP_CW0 + 1:_HP_CW0 + 2, :] * h1
                    + hp[_HP_CW0 + 2:_HP_CW0 + 3, :] * h)
            prev = h[ROW_HALF - 8:, :]
            zb = _silu(_dot(xb16, wzb[...]))
            bb = _dot(xb16, wbb[...])
            yb_ref[s, rows, :] = (bb * (conv * zb)).astype(_BF16)
        hcarry_ref[s] = prev

    def phase2(seed, last):
        wga, wgb = w0, w1
        n_pass = tm // P2_ROWS
        gate_bias = bg_ref[s - N_HEADS]

        def merge_pass(r):
            rows = slice(r * P2_ROWS, (r + 1) * P2_ROWS)
            if seed:
                acc[rows, :] = DN_ALPHA * xbuf[rows, :]
            xb16 = xbf_ref[rows, :]
            a = _dot(ya_ref[0, rows, :], woa[0:HEAD_DIM, :].astype(_BF16))
            b = _dot(yb_ref[0, rows, :], wob[0:HEAD_DIM, :].astype(_BF16))
            for k in range(1, N_HEADS):
                a += _dot(ya_ref[k, rows, :], woa[k * HEAD_DIM:(k + 1) * HEAD_DIM, :].astype(_BF16))
                b += _dot(yb_ref[k, rows, :], wob[k * HEAD_DIM:(k + 1) * HEAD_DIM, :].astype(_BF16))
            ga = _sigmoid(_dot(xb16, wga[...]) + gate_bias[0:1, :])
            gb = _sigmoid(_dot(xb16, wgb[...]) + gate_bias[1:2, :])
            return (ga * a + gb * b).astype(_BF16)

        def out_pass(r, merged):
            rows = slice(r * P2_ROWS, (r + 1) * P2_ROWS)
            acc[rows, :] += _dot(merged, wout_ref[...].astype(_BF16))
            if last:
                y = acc[rows, :]
                mu = jnp.mean(y, axis=-1, keepdims=True)
                yc = y - mu
                var = jnp.mean(yc * yc, axis=-1, keepdims=True)
                acc[rows, :] = yc * lax.rsqrt(var + LN_EPS) * ln_ref[0:1, :] + ln_ref[1:2, :]

        if last:
            for r in range(n_pass):
                out_pass(r, merge_pass(r))
        else:
            merged = [merge_pass(r) for r in range(n_pass)]
            for r in range(n_pass):
                out_pass(r, merged[r])

    @pl.when(s == 0)
    def _first_head():
        @pl.when(i == 0)
        def _():
            x_copy(i).start()
        x_copy(i).wait()
        phase1(True)

    @pl.when(jnp.logical_and(s > 0, s < N_HEADS))
    def _other_heads():
        phase1(False)

    @pl.when(s == N_HEADS)
    def _first_col():
        @pl.when(i > 0)
        def _():
            out_copy(i - 1).wait()
        phase2(True, False)

        @pl.when(i + 1 < n_tiles)
        def _():
            x_copy(i + 1).start()

    @pl.when(jnp.logical_and(s > N_HEADS, s < 2 * N_HEADS - 1))
    def _other_cols():
        phase2(False, False)

    @pl.when(s == 2 * N_HEADS - 1)
    def _last_col():
        phase2(False, True)
        out_copy(i).start()

        @pl.when(i == n_tiles - 1)
        def _():
            out_copy(i).wait()


def _layer(x2d, w_in_bf, w_oa_bf, w_ob_bf, w_out_bf, head_params, w_s, b_s_col, gate_bias, ln_params,
           *, seq_len):
    m = x2d.shape[0]
    tm = TOKEN_TILE
    assert m % tm == 0 and seq_len % tm == 0 and tm % ROW_HALF == 0 and ROW_HALF % CHUNK == 0
    n_tiles = m // tm
    last = N_HEADS - 1

    def head_of(s):
        return jnp.minimum(s, last)

    def col_of(s):
        return jnp.maximum(s - N_HEADS, 0)

    def w_slot(k):
        if k < len(_SLOT_P2):
            def idx(i, s, k=k):
                return (0, jnp.where(s < N_HEADS, _SLOT_P1[k] + s, _SLOT_P2[k] + col_of(s)))
        else:
            def idx(i, s, k=k):
                return (0, _SLOT_P1[k] + head_of(s))
        return pl.BlockSpec((D_MODEL, HEAD_DIM), idx)

    in_specs = [pl.BlockSpec(memory_space=pl.ANY)]
    in_specs += [w_slot(k) for k in range(len(_SLOT_P1))]
    in_specs += [
        pl.BlockSpec((D_MODEL, HEAD_DIM), lambda i, s: (0, col_of(s))),
        pl.BlockSpec((D_MODEL, HEAD_DIM), lambda i, s: (0, col_of(s))),
        pl.BlockSpec((HEAD_DIM, D_MODEL), lambda i, s: (col_of(s), 0)),
        pl.BlockSpec((N_HEADS, 8, HEAD_DIM), lambda i, s: (0, 0, 0)),
        pl.BlockSpec((N_HEADS, CHUNK, CHUNK), lambda i, s: (0, 0, 0)),
        pl.BlockSpec((N_HEADS, CHUNK, 1), lambda i, s: (0, 0, 0)),
        pl.BlockSpec((N_HEADS, 2, HEAD_DIM), lambda i, s: (0, 0, 0)),
        pl.BlockSpec((2, D_MODEL), lambda i, s: (0, 0)),
    ]
    kern = functools.partial(_fused_kernel, tiles_per_seq=seq_len // tm, n_tiles=n_tiles)
    return pl.pallas_call(
        kern,
        grid=(n_tiles, 2 * N_HEADS),
        in_specs=in_specs,
        out_specs=pl.BlockSpec(memory_space=pl.ANY),
        out_shape=jax.ShapeDtypeStruct((m, D_MODEL), _F32),
        scratch_shapes=[
            pltpu.VMEM((tm, D_MODEL), _F32),
            pltpu.VMEM((tm, D_MODEL), _F32),
            pltpu.VMEM((tm, D_MODEL), _BF16),
            pltpu.VMEM((N_HEADS, tm, HEAD_DIM), _BF16),
            pltpu.VMEM((N_HEADS, tm, HEAD_DIM), _BF16),
            pltpu.VMEM((N_HEADS, 8, HEAD_DIM), _F32),
            pltpu.SemaphoreType.DMA((2,)),
        ],
        compiler_params=pltpu.CompilerParams(
            dimension_semantics=("arbitrary", "arbitrary"),
            vmem_limit_bytes=VMEM_LIMIT_BYTES),
        name="fused_layer",
    )(x2d, *([w_in_bf] * len(_SLOT_P1)), w_oa_bf, w_ob_bf, w_out_bf,
      head_params, w_s, b_s_col, gate_bias, ln_params)


def kernel(x, w_in, b_gate, ln_v_g, ln_v_b, w_s, b_s, conv_w, conv_b, w_oa, w_ob, w_out, ln_g, ln_b):
    bsz, seq, d = x.shape
    depth = w_in.shape[0]
    x2d = x.reshape(bsz * seq, d)
    for l in range(depth):
        head_params = jnp.concatenate(
            [ln_v_g[l][None], ln_v_b[l][None], conv_b[l][None], conv_w[l],
             jnp.zeros((8 - 3 - CONV_WIDTH, d), _F32)], axis=0)
        head_params = head_params.reshape(8, N_HEADS, HEAD_DIM).transpose(1, 0, 2)
        gate_bias = b_gate[l].reshape(2, N_HEADS, HEAD_DIM).transpose(1, 0, 2)
        x2d = _layer(
            x2d, w_in[l].astype(_BF16), w_oa[l], w_ob[l], w_out[l], head_params,
            w_s[l], b_s[l].reshape(N_HEADS, CHUNK, 1),
            gate_bias, jnp.stack([ln_g[l], ln_b[l]]),
            seq_len=seq)
    return x2d.reshape(bsz, seq, d)
```

```python
import functools

import jax
import jax.numpy as jnp
from jax import lax
from jax.experimental import pallas as pl
from jax.experimental.pallas import tpu as pltpu

D_MODEL = 2048
CHUNK = 128
HEAD_DIM = 256
N_HEADS = D_MODEL // HEAD_DIM
CONV_WIDTH = 3
DN_ALPHA = 2.0 ** 0.25
LN_EPS = 1e-5

TOKEN_TILE = 1024
ROW_HALF = 256
P2_ROWS = 256
VMEM_LIMIT_BYTES = 60 * 1024 * 1024

_G_U, _G_V, _G_ZA, _G_XB, _G_CB, _G_BB, _G_ZB, _G_GA, _G_GB = (g * N_HEADS for g in range(9))
_SLOT_P1 = (_G_V, _G_U, _G_ZA, _G_CB, _G_XB, _G_BB, _G_ZB)
_SLOT_P2 = (_G_GA, _G_GB)
_HP_LNG, _HP_LNB, _HP_CB, _HP_CW0 = 0, 1, 2, 3

_F32 = jnp.float32
_BF16 = jnp.bfloat16


def _dot(a, b):
    return jnp.dot(a, b, preferred_element_type=_F32)


def _sigmoid(t):
    return 0.5 * jnp.tanh(0.5 * t) + 0.5


def _silu(t):
    return t * _sigmoid(t)


def _fused_kernel(x_hbm, w0, w1, w2, w3, w4, w5, w6, woa, wob, wout_ref,
                  hp_ref, ws_ref, bs_ref, bg_ref, ln_ref,
                  o_hbm,
                  xbuf, acc, xbf_ref, ya_ref, yb_ref, hcarry_ref, sem,
                  *, tiles_per_seq, n_tiles):
    i = pl.program_id(0)
    s = pl.program_id(1)
    tm = xbuf.shape[0]
    n_half = tm // ROW_HALF

    def x_copy(tile):
        return pltpu.make_async_copy(x_hbm.at[pl.ds(tile * tm, tm), :], xbuf, sem.at[0])

    def out_copy(tile):
        return pltpu.make_async_copy(acc, o_hbm.at[pl.ds(tile * tm, tm), :], sem.at[1])

    def phase1(first):
        wv, wu, wza, wcb, wxb, wbb, wzb = w0, w1, w2, w3, w4, w5, w6
        row = lax.broadcasted_iota(jnp.int32, (CHUNK, CHUNK), 0)
        col = lax.broadcasted_iota(jnp.int32, (CHUNK, CHUNK), 1)
        w_mix = jnp.where(row >= col, ws_ref[s], 0.0).astype(_BF16)
        bias_col = bs_ref[s]
        hp = hp_ref[s]

        @pl.when((i % tiles_per_seq) == 0)
        def _reset_history():
            hcarry_ref[s] = jnp.zeros((8, HEAD_DIM), _F32)

        prev = hcarry_ref[s]
        for r in range(n_half):
            rows = slice(r * ROW_HALF, (r + 1) * ROW_HALF)
            if first:
                xbf_ref[rows, :] = xbuf[rows, :].astype(_BF16)
            xb16 = xbf_ref[rows, :]
            v = jax.nn.gelu(_dot(xb16, wv[...]))
            u = jax.nn.gelu(_dot(xb16, wu[...]))
            za = _silu(_dot(xb16, wza[...]))
            h = _dot(xb16, wcb[...]) * _dot(xb16, wxb[...])

            mu = jnp.mean(v, axis=-1, keepdims=True)
            vc = v - mu
            var = jnp.mean(vc * vc, axis=-1, keepdims=True)
            vn = (vc * lax.rsqrt(var + LN_EPS) * hp[_HP_LNG:_HP_LNG + 1, :]
                  + hp[_HP_LNB:_HP_LNB + 1, :])
            vn16 = vn.astype(_BF16)
            mixed = jnp.concatenate(
                [_dot(w_mix, vn16[c * CHUNK:(c + 1) * CHUNK, :]) + bias_col
                 for c in range(ROW_HALF // CHUNK)], axis=0)
            ya_ref[s, rows, :] = (u * mixed * za).astype(_BF16)

            rid = lax.broadcasted_iota(jnp.int32, h.shape, 0)
            h1 = jnp.where(rid == 0, prev[7:8, :], pltpu.roll(h, 1, 0))
            h2 = jnp.where(rid == 0, prev[6:7, :],
                           jnp.where(rid == 1, prev[7:8, :], pltpu.roll(h, 2, 0)))
            conv = (hp[_HP_CB:_HP_CB + 1, :]
                    + hp[_HP_CW0:_HP_CW0 + 1, :] * h2
                    + hp[_HP_CW0 + 1:_HP_CW0 + 2, :] * h1
                    + hp[_HP_CW0 + 2:_HP_CW0 + 3, :] * h)
            prev = h[ROW_HALF - 8:, :]
            zb = _silu(_dot(xb16, wzb[...]))
            bb = _dot(xb16, wbb[...])
            yb_ref[s, rows, :] = (bb * (conv * zb)).astype(_BF16)
        hcarry_ref[s] = prev

    def phase2(seed, last):
        wga, wgb = w0, w1
        n_pass = tm // P2_ROWS
        gate_bias = bg_ref[s - N_HEADS]

        def merge_pass(r):
            rows = slice(r * P2_ROWS, (r + 1) * P2_ROWS)
            if seed:
                acc[rows, :] = DN_ALPHA * xbuf[rows, :]
            xb16 = xbf_ref[rows, :]
            a = _dot(ya_ref[0, rows, :], woa[0:HEAD_DIM, :].astype(_BF16))
            b = _dot(yb_ref[0, rows, :], wob[0:HEAD_DIM, :].astype(_BF16))
            for k in range(1, N_HEADS):
                a += _dot(ya_ref[k, rows, :], woa[k * HEAD_DIM:(k + 1) * HEAD_DIM, :].astype(_BF16))
                b += _dot(yb_ref[k, rows, :], wob[k * HEAD_DIM:(k + 1) * HEAD_DIM, :].astype(_BF16))
            ga = _sigmoid(_dot(xb16, wga[...]) + gate_bias[0:1, :])
            gb = _sigmoid(_dot(xb16, wgb[...]) + gate_bias[1:2, :])
            return (ga * a + gb * b).astype(_BF16)

        def out_pass(r, merged):
            rows = slice(r * P2_ROWS, (r + 1) * P2_ROWS)
            acc[rows, :] += _dot(merged, wout_ref[...].astype(_BF16))
            if last:
                y = acc[rows, :]
                mu = jnp.mean(y, axis=-1, keepdims=True)
                yc = y - mu
                var = jnp.mean(yc * yc, axis=-1, keepdims=True)
                acc[rows, :] = yc * lax.rsqrt(var + LN_EPS) * ln_ref[0:1, :] + ln_ref[1:2, :]

        if last:
            for r in range(n_pass):
                out_pass(r, merge_pass(r))
        else:
            merged = [merge_pass(r) for r in range(n_pass)]
            for r in range(n_pass):
                out_pass(r, merged[r])

    @pl.when(s == 0)
    def _first_head():
        @pl.when(i == 0)
        def _():
            x_copy(i).start()
        x_copy(i).wait()
        phase1(True)

    @pl.when(jnp.logical_and(s > 0, s < N_HEADS))
    def _other_heads():
        phase1(False)

    @pl.when(s == N_HEADS)
    def _first_col():
        @pl.when(i > 0)
        def _():
            out_copy(i - 1).wait()
        phase2(True, False)

        @pl.when(i + 1 < n_tiles)
        def _():
            x_copy(i + 1).start()

    @pl.when(jnp.logical_and(s > N_HEADS, s < 2 * N_HEADS - 1))
    def _other_cols():
        phase2(False, False)

    @pl.when(s == 2 * N_HEADS - 1)
    def _last_col():
        phase2(False, True)
        out_copy(i).start()

        @pl.when(i == n_tiles - 1)
        def _():
            out_copy(i).wait()


def _layer(x2d, w_in_bf, w_oa_bf, w_ob_bf, w_out_bf, head_params, w_s, b_s_col, gate_bias, ln_params,
           *, seq_len):
    m = x2d.shape[0]
    tm = TOKEN_TILE
    assert m % tm == 0 and seq_len % tm == 0 and tm % ROW_HALF == 0 and ROW_HALF % CHUNK == 0
    n_tiles = m // tm
    last = N_HEADS - 1

    def head_of(s):
        return jnp.minimum(s, last)

    def col_of(s):
        return jnp.maximum(s - N_HEADS, 0)

    def w_slot(k):
        if k < len(_SLOT_P2):
            def idx(i, s, k=k):
                return (0, jnp.where(s < N_HEADS, _SLOT_P1[k] + s, _SLOT_P2[k] + col_of(s)))
        else:
            def idx(i, s, k=k):
                return (0, _SLOT_P1[k] + head_of(s))
        return pl.BlockSpec((D_MODEL, HEAD_DIM), idx)

    in_specs = [pl.BlockSpec(memory_space=pl.ANY)]
    in_specs += [w_slot(k) for k in range(len(_SLOT_P1))]
    in_specs += [
        pl.BlockSpec((D_MODEL, HEAD_DIM), lambda i, s: (0, col_of(s))),
        pl.BlockSpec((D_MODEL, HEAD_DIM), lambda i, s: (0, col_of(s))),
        pl.BlockSpec((HEAD_DIM, D_MODEL), lambda i, s: (col_of(s), 0)),
        pl.BlockSpec((N_HEADS, 8, HEAD_DIM), lambda i, s: (0, 0, 0)),
        pl.BlockSpec((N_HEADS, CHUNK, CHUNK), lambda i, s: (0, 0, 0)),
        pl.BlockSpec((N_HEADS, CHUNK, 1), lambda i, s: (0, 0, 0)),
        pl.BlockSpec((N_HEADS, 2, HEAD_DIM), lambda i, s: (0, 0, 0)),
        pl.BlockSpec((2, D_MODEL), lambda i, s: (0, 0)),
    ]
    kern = functools.partial(_fused_kernel, tiles_per_seq=seq_len // tm, n_tiles=n_tiles)
    return pl.pallas_call(
        kern,
        grid=(n_tiles, 2 * N_HEADS),
        in_specs=in_specs,
        out_specs=pl.BlockSpec(memory_space=pl.ANY),
        out_shape=jax.ShapeDtypeStruct((m, D_MODEL), _F32),
        scratch_shapes=[
            pltpu.VMEM((tm, D_MODEL), _F32),
            pltpu.VMEM((tm, D_MODEL), _F32),
            pltpu.VMEM((tm, D_MODEL), _BF16),
            pltpu.VMEM((N_HEADS, tm, HEAD_DIM), _BF16),
            pltpu.VMEM((N_HEADS, tm, HEAD_DIM), _BF16),
            pltpu.VMEM((N_HEADS, 8, HEAD_DIM), _F32),
            pltpu.SemaphoreType.DMA((2,)),
        ],
        compiler_params=pltpu.CompilerParams(
            dimension_semantics=("arbitrary", "arbitrary"),
            vmem_limit_bytes=VMEM_LIMIT_BYTES),
        name="fused_layer",
    )(x2d, *([w_in_bf] * len(_SLOT_P1)), w_oa_bf, w_ob_bf, w_out_bf,
      head_params, w_s, b_s_col, gate_bias, ln_params)


def kernel(x, w_in, b_gate, ln_v_g, ln_v_b, w_s, b_s, conv_w, conv_b, w_oa, w_ob, w_out, ln_g, ln_b):
    bsz, seq, d = x.shape
    depth = w_in.shape[0]
    x2d = x.reshape(bsz * seq, d)
    for l in range(depth):
        head_params = jnp.concatenate(
            [ln_v_g[l][None], ln_v_b[l][None], conv_b[l][None], conv_w[l],
             jnp.zeros((8 - 3 - CONV_WIDTH, d), _F32)], axis=0)
        head_params = head_params.reshape(8, N_HEADS, HEAD_DIM).transpose(1, 0, 2)
        gate_bias = b_gate[l].reshape(2, N_HEADS, HEAD_DIM).transpose(1, 0, 2)
        x2d = _layer(
            x2d, w_in[l].astype(_BF16), w_oa[l], w_ob[l], w_out[l], head_params,
            w_s[l], b_s[l].reshape(N_HEADS, CHUNK, 1),
            gate_bias, jnp.stack([ln_g[l], ln_b[l]]),
            seq_len=seq)
    return x2d.reshape(bsz, seq, d)
```

```python
import functools

import jax
import jax.numpy as jnp
from jax import lax
from jax.experimental import pallas as pl
from jax.experimental.pallas import tpu as pltpu

D_MODEL = 2048
CHUNK = 128
HEAD_DIM = 256
N_HEADS = D_MODEL // HEAD_DIM
CONV_WIDTH = 3
DN_ALPHA = 2.0 ** 0.25
LN_EPS = 1e-5

SUBLANES = 8
TOKEN_TILE = 1024
P1_ROWS = 256
P2_ROWS = 256
VMEM_LIMIT_BYTES = 60 * 1024 * 1024

_G_U, _G_V, _G_ZA, _G_XB, _G_CB, _G_BB, _G_ZB, _G_GA, _G_GB = (g * N_HEADS for g in range(9))
_SLOT_P1 = (_G_V, _G_U, _G_ZA, _G_CB, _G_XB, _G_BB, _G_ZB)
_SLOT_P2 = (_G_GA, _G_GB)
_HP_LNG, _HP_LNB, _HP_CB, _HP_CW0 = 0, 1, 2, 3
_HP_ROWS = SUBLANES
assert _HP_CW0 + CONV_WIDTH <= _HP_ROWS

_F32 = jnp.float32
_BF16 = jnp.bfloat16


def _dot(a, b):
    return jnp.dot(a, b, preferred_element_type=_F32)


def _sigmoid(t):
    return 0.5 * jnp.tanh(0.5 * t) + 0.5


def _silu(t):
    return t * _sigmoid(t)


def _fused_kernel(x_hbm, w0, w1, w2, w3, w4, w5, w6, woa, wob, wout_ref,
                  hp_ref, ws_ref, bs_ref, bg_ref, ln_ref,
                  o_hbm,
                  xbuf, acc, xbf_ref, ya_ref, yb_ref, hcarry_ref, sem,
                  *, tiles_per_seq, n_tiles):
    i = pl.program_id(0)
    s = pl.program_id(1)
    tm = xbuf.shape[0]

    def x_copy(tile):
        return pltpu.make_async_copy(x_hbm.at[pl.ds(tile * tm, tm), :], xbuf, sem.at[0])

    def out_copy(tile):
        return pltpu.make_async_copy(acc, o_hbm.at[pl.ds(tile * tm, tm), :], sem.at[1])

    def phase1(first):
        wv, wu, wza, wcb, wxb, wbb, wzb = w0, w1, w2, w3, w4, w5, w6
        row = lax.broadcasted_iota(jnp.int32, (CHUNK, CHUNK), 0)
        col = lax.broadcasted_iota(jnp.int32, (CHUNK, CHUNK), 1)
        w_mix = jnp.where(row >= col, ws_ref[s], 0.0).astype(_BF16)
        bias_col = bs_ref[s]
        hp = hp_ref[s]

        @pl.when((i % tiles_per_seq) == 0)
        def _reset_history():
            hcarry_ref[s] = jnp.zeros((SUBLANES, HEAD_DIM), _F32)

        prev = hcarry_ref[s]
        for r in range(tm // P1_ROWS):
            rows = slice(r * P1_ROWS, (r + 1) * P1_ROWS)
            if first:
                xbf_ref[rows, :] = xbuf[rows, :].astype(_BF16)
            xb16 = xbf_ref[rows, :]
            v = jax.nn.gelu(_dot(xb16, wv[...]))
            u = jax.nn.gelu(_dot(xb16, wu[...]))
            za = _silu(_dot(xb16, wza[...]))
            h = _dot(xb16, wcb[...]) * _dot(xb16, wxb[...])

            mu = jnp.mean(v, axis=-1, keepdims=True)
            vc = v - mu
            var = jnp.mean(vc * vc, axis=-1, keepdims=True)
            vn = (vc * lax.rsqrt(var + LN_EPS) * hp[_HP_LNG:_HP_LNG + 1, :]
                  + hp[_HP_LNB:_HP_LNB + 1, :])
            vn16 = vn.astype(_BF16)
            mixed = jnp.concatenate(
                [_dot(w_mix, vn16[c * CHUNK:(c + 1) * CHUNK, :]) + bias_col
                 for c in range(P1_ROWS // CHUNK)], axis=0)
            ya_ref[s, rows, :] = (u * mixed * za).astype(_BF16)

            rid = lax.broadcasted_iota(jnp.int32, h.shape, 0)
            last1 = prev[SUBLANES - 1:SUBLANES, :]
            last2 = prev[SUBLANES - 2:SUBLANES - 1, :]
            h1 = jnp.where(rid == 0, last1, pltpu.roll(h, 1, 0))
            h2 = jnp.where(rid == 0, last2, jnp.where(rid == 1, last1, pltpu.roll(h, 2, 0)))
            conv = (hp[_HP_CB:_HP_CB + 1, :]
                    + hp[_HP_CW0:_HP_CW0 + 1, :] * h2
                    + hp[_HP_CW0 + 1:_HP_CW0 + 2, :] * h1
                    + hp[_HP_CW0 + 2:_HP_CW0 + 3, :] * h)
            prev = h[P1_ROWS - SUBLANES:, :]
            zb = _silu(_dot(xb16, wzb[...]))
            bb = _dot(xb16, wbb[...])
            yb_ref[s, rows, :] = (bb * (conv * zb)).astype(_BF16)
        hcarry_ref[s] = prev

    def phase2(seed, last):
        wga, wgb = w0, w1
        n_pass = tm // P2_ROWS
        gate_bias = bg_ref[s - N_HEADS]

        def merge_pass(r):
            rows = slice(r * P2_ROWS, (r + 1) * P2_ROWS)
            if seed:
                acc[rows, :] = DN_ALPHA * xbuf[rows, :]
            xb16 = xbf_ref[rows, :]
            ga = _sigmoid(_dot(xb16, wga[...]) + gate_bias[0:1, :])
            gb = _sigmoid(_dot(xb16, wgb[...]) + gate_bias[1:2, :])
            a = _dot(ya_ref[0, rows, :], woa[0:HEAD_DIM, :].astype(_BF16))
            b = _dot(yb_ref[0, rows, :], wob[0:HEAD_DIM, :].astype(_BF16))
            for k in range(1, N_HEADS):
                a += _dot(ya_ref[k, rows, :], woa[k * HEAD_DIM:(k + 1) * HEAD_DIM, :].astype(_BF16))
                b += _dot(yb_ref[k, rows, :], wob[k * HEAD_DIM:(k + 1) * HEAD_DIM, :].astype(_BF16))
            return (ga * a + gb * b).astype(_BF16)

        def out_pass(r, merged):
            rows = slice(r * P2_ROWS, (r + 1) * P2_ROWS)
            acc[rows, :] += _dot(merged, wout_ref[...].astype(_BF16))
            if last:
                y = acc[rows, :]
                mu = jnp.mean(y, axis=-1, keepdims=True)
                yc = y - mu
                var = jnp.mean(yc * yc, axis=-1, keepdims=True)
                acc[rows, :] = yc * lax.rsqrt(var + LN_EPS) * ln_ref[0:1, :] + ln_ref[1:2, :]

        if last:
            for r in range(n_pass):
                out_pass(r, merge_pass(r))
        else:
            merged = [merge_pass(r) for r in range(n_pass)]
            for r in range(n_pass):
                out_pass(r, merged[r])

    @pl.when(s == 0)
    def _first_head():
        @pl.when(i == 0)
        def _():
            x_copy(i).start()
        x_copy(i).wait()
        phase1(True)

    @pl.when(jnp.logical_and(s > 0, s < N_HEADS))
    def _other_heads():
        phase1(False)

    @pl.when(s == N_HEADS)
    def _first_col():
        @pl.when(i > 0)
        def _():
            out_copy(i - 1).wait()
        phase2(True, False)

        @pl.when(i + 1 < n_tiles)
        def _():
            x_copy(i + 1).start()

    @pl.when(jnp.logical_and(s > N_HEADS, s < 2 * N_HEADS - 1))
    def _other_cols():
        phase2(False, False)

    @pl.when(s == 2 * N_HEADS - 1)
    def _last_col():
        phase2(False, True)
        out_copy(i).start()

        @pl.when(i == n_tiles - 1)
        def _():
            out_copy(i).wait()


def _layer(x2d, w_in_bf, w_oa, w_ob, w_out, head_params, w_s, b_s_col, gate_bias, ln_params, *, seq_len):
    m, d = x2d.shape
    tm = TOKEN_TILE
    assert d == D_MODEL and w_in_bf.shape == (D_MODEL, 9 * D_MODEL)
    assert m % tm == 0 and seq_len % tm == 0
    assert tm % P1_ROWS == 0 and tm % P2_ROWS == 0 and P1_ROWS % CHUNK == 0
    n_tiles = m // tm
    last = N_HEADS - 1

    def head_of(s):
        return jnp.minimum(s, last)

    def col_of(s):
        return jnp.maximum(s - N_HEADS, 0)

    def w_slot(k):
        if k < len(_SLOT_P2):
            def idx(i, s, k=k):
                return (0, jnp.where(s < N_HEADS, _SLOT_P1[k] + s, _SLOT_P2[k] + col_of(s)))
        else:
            def idx(i, s, k=k):
                return (0, _SLOT_P1[k] + head_of(s))
        return pl.BlockSpec((D_MODEL, HEAD_DIM), idx)

    in_specs = [pl.BlockSpec(memory_space=pl.ANY)]
    in_specs += [w_slot(k) for k in range(len(_SLOT_P1))]
    in_specs += [
        pl.BlockSpec((D_MODEL, HEAD_DIM), lambda i, s: (0, col_of(s))),
        pl.BlockSpec((D_MODEL, HEAD_DIM), lambda i, s: (0, col_of(s))),
        pl.BlockSpec((HEAD_DIM, D_MODEL), lambda i, s: (col_of(s), 0)),
        pl.BlockSpec((N_HEADS, _HP_ROWS, HEAD_DIM), lambda i, s: (0, 0, 0)),
        pl.BlockSpec((N_HEADS, CHUNK, CHUNK), lambda i, s: (0, 0, 0)),
        pl.BlockSpec((N_HEADS, CHUNK, 1), lambda i, s: (0, 0, 0)),
        pl.BlockSpec((N_HEADS, 2, HEAD_DIM), lambda i, s: (0, 0, 0)),
        pl.BlockSpec((2, D_MODEL), lambda i, s: (0, 0)),
    ]
    kern = functools.partial(_fused_kernel, tiles_per_seq=seq_len // tm, n_tiles=n_tiles)
    return pl.pallas_call(
        kern,
        grid=(n_tiles, 2 * N_HEADS),
        in_specs=in_specs,
        out_specs=pl.BlockSpec(memory_space=pl.ANY),
        out_shape=jax.ShapeDtypeStruct((m, D_MODEL), _F32),
        scratch_shapes=[
            pltpu.VMEM((tm, D_MODEL), _F32),
            pltpu.VMEM((tm, D_MODEL), _F32),
            pltpu.VMEM((tm, D_MODEL), _BF16),
            pltpu.VMEM((N_HEADS, tm, HEAD_DIM), _BF16),
            pltpu.VMEM((N_HEADS, tm, HEAD_DIM), _BF16),
            pltpu.VMEM((N_HEADS, SUBLANES, HEAD_DIM), _F32),
            pltpu.SemaphoreType.DMA((2,)),
        ],
        compiler_params=pltpu.CompilerParams(
            dimension_semantics=("arbitrary", "arbitrary"),
            vmem_limit_bytes=VMEM_LIMIT_BYTES),
        name="fused_layer",
    )(x2d, *([w_in_bf] * len(_SLOT_P1)), w_oa, w_ob, w_out,
      head_params, w_s, b_s_col, gate_bias, ln_params)


def kernel(x, w_in, b_gate, ln_v_g, ln_v_b, w_s, b_s, conv_w, conv_b, w_oa, w_ob, w_out, ln_g, ln_b):
    bsz, seq, d = x.shape
    depth = w_in.shape[0]
    x2d = x.reshape(bsz * seq, d)
    for l in range(depth):
        head_params = jnp.concatenate(
            [ln_v_g[l][None], ln_v_b[l][None], conv_b[l][None], conv_w[l],
             jnp.zeros((_HP_ROWS - _HP_CW0 - CONV_WIDTH, d), _F32)], axis=0)
        head_params = head_params.reshape(_HP_ROWS, N_HEADS, HEAD_DIM).transpose(1, 0, 2)
        gate_bias = b_gate[l].reshape(2, N_HEADS, HEAD_DIM).transpose(1, 0, 2)
        x2d = _layer(
            x2d, w_in[l].astype(_BF16), w_oa[l], w_ob[l], w_out[l], head_params,
            w_s[l], b_s[l].reshape(N_HEADS, CHUNK, 1),
            gate_bias, jnp.stack([ln_g[l], ln_b[l]]),
            seq_len=seq)
    return x2d.reshape(bsz, seq, d)
```

```python
import functools

import jax
import jax.numpy as jnp
from jax import lax
from jax.experimental import pallas as pl
from jax.experimental.pallas import tpu as pltpu

D_MODEL = 2048
CHUNK = 128
HEAD_DIM = 256
N_HEADS = D_MODEL // HEAD_DIM
CONV_WIDTH = 3
DN_ALPHA = 2.0 ** 0.25
LN_EPS = 1e-5

SUBLANES = 8
TOKEN_TILE = 1024
P1_ROWS = 1024
P2_ROWS = 256
VMEM_LIMIT_BYTES = 64 * 1024 * 1024

_G_U, _G_V, _G_ZA, _G_XB, _G_CB, _G_BB, _G_ZB, _G_GA, _G_GB = (g * N_HEADS for g in range(9))
_SLOT_P1 = (_G_V, _G_U, _G_ZA, _G_CB, _G_XB, _G_BB, _G_ZB)
_SLOT_P2 = (_G_GA, _G_GB)
_HP_LNG, _HP_LNB, _HP_CB, _HP_CW0 = 0, 1, 2, 3
_HP_ROWS = SUBLANES
assert _HP_CW0 + CONV_WIDTH <= _HP_ROWS

_F32 = jnp.float32
_BF16 = jnp.bfloat16


def _dot(a, b):
    return jnp.dot(a, b, preferred_element_type=_F32)


def _sigmoid(t):
    return 0.5 * jnp.tanh(0.5 * t) + 0.5


def _silu(t):
    return t * _sigmoid(t)


def _fused_kernel(x_hbm, w0, w1, w2, w3, w4, w5, w6, woa, wob, wout_ref,
                  hp_ref, ws_ref, bs_ref, bg_ref, ln_ref,
                  o_hbm,
                  xbuf, acc, xbf_ref, ya_ref, yb_ref, hcarry_ref, sem,
                  *, tiles_per_seq, n_tiles):
    i = pl.program_id(0)
    s = pl.program_id(1)
    tm = xbuf.shape[0]

    def x_copy(tile):
        return pltpu.make_async_copy(x_hbm.at[pl.ds(tile * tm, tm), :], xbuf, sem.at[0])

    def out_copy(tile):
        return pltpu.make_async_copy(acc, o_hbm.at[pl.ds(tile * tm, tm), :], sem.at[1])

    def phase1(first):
        wv, wu, wza, wcb, wxb, wbb, wzb = w0, w1, w2, w3, w4, w5, w6
        row = lax.broadcasted_iota(jnp.int32, (CHUNK, CHUNK), 0)
        col = lax.broadcasted_iota(jnp.int32, (CHUNK, CHUNK), 1)
        w_mix = jnp.where(row >= col, ws_ref[s], 0.0).astype(_BF16)
        bias_col = bs_ref[s]
        hp = hp_ref[s]

        @pl.when((i % tiles_per_seq) == 0)
        def _reset_history():
            hcarry_ref[s] = jnp.zeros((SUBLANES, HEAD_DIM), _F32)

        prev = hcarry_ref[s]
        for r in range(tm // P1_ROWS):
            rows = slice(r * P1_ROWS, (r + 1) * P1_ROWS)
            if first:
                xbf_ref[rows, :] = xbuf[rows, :].astype(_BF16)
            xb16 = xbf_ref[rows, :]
            v = jax.nn.gelu(_dot(xb16, wv[...]))
            u = jax.nn.gelu(_dot(xb16, wu[...]))
            za = _silu(_dot(xb16, wza[...]))
            h = _dot(xb16, wcb[...]) * _dot(xb16, wxb[...])

            mu = jnp.mean(v, axis=-1, keepdims=True)
            vc = v - mu
            var = jnp.mean(vc * vc, axis=-1, keepdims=True)
            vn = (vc * lax.rsqrt(var + LN_EPS) * hp[_HP_LNG:_HP_LNG + 1, :]
                  + hp[_HP_LNB:_HP_LNB + 1, :])
            vn16 = vn.astype(_BF16)
            mixed = jnp.concatenate(
                [_dot(w_mix, vn16[c * CHUNK:(c + 1) * CHUNK, :]) + bias_col
                 for c in range(P1_ROWS // CHUNK)], axis=0)
            ya_ref[s, rows, :] = (u * mixed * za).astype(_BF16)

            rid = lax.broadcasted_iota(jnp.int32, h.shape, 0)
            last1 = prev[SUBLANES - 1:SUBLANES, :]
            last2 = prev[SUBLANES - 2:SUBLANES - 1, :]
            h1 = jnp.where(rid == 0, last1, pltpu.roll(h, 1, 0))
            h2 = jnp.where(rid == 0, last2, jnp.where(rid == 1, last1, pltpu.roll(h, 2, 0)))
            conv = (hp[_HP_CB:_HP_CB + 1, :]
                    + hp[_HP_CW0:_HP_CW0 + 1, :] * h2
                    + hp[_HP_CW0 + 1:_HP_CW0 + 2, :] * h1
                    + hp[_HP_CW0 + 2:_HP_CW0 + 3, :] * h)
            prev = h[P1_ROWS - SUBLANES:, :]
            zb = _silu(_dot(xb16, wzb[...]))
            bb = _dot(xb16, wbb[...])
            yb_ref[s, rows, :] = (bb * (conv * zb)).astype(_BF16)
        hcarry_ref[s] = prev

    def phase2(seed, last):
        wga, wgb = w0, w1
        n_pass = tm // P2_ROWS
        gate_bias = bg_ref[s - N_HEADS]

        def merge_pass(r):
            rows = slice(r * P2_ROWS, (r + 1) * P2_ROWS)
            if seed:
                acc[rows, :] = DN_ALPHA * xbuf[rows, :]
            xb16 = xbf_ref[rows, :]
            ga = _sigmoid(_dot(xb16, wga[...]) + gate_bias[0:1, :])
            gb = _sigmoid(_dot(xb16, wgb[...]) + gate_bias[1:2, :])
            a = _dot(ya_ref[0, rows, :], woa[0:HEAD_DIM, :].astype(_BF16))
            b = _dot(yb_ref[0, rows, :], wob[0:HEAD_DIM, :].astype(_BF16))
            for k in range(1, N_HEADS):
                a += _dot(ya_ref[k, rows, :], woa[k * HEAD_DIM:(k + 1) * HEAD_DIM, :].astype(_BF16))
                b += _dot(yb_ref[k, rows, :], wob[k * HEAD_DIM:(k + 1) * HEAD_DIM, :].astype(_BF16))
            return (ga * a + gb * b).astype(_BF16)

        def out_pass(r, merged):
            rows = slice(r * P2_ROWS, (r + 1) * P2_ROWS)
            acc[rows, :] += _dot(merged, wout_ref[...].astype(_BF16))
            if last:
                y = acc[rows, :]
                mu = jnp.mean(y, axis=-1, keepdims=True)
                yc = y - mu
                var = jnp.mean(yc * yc, axis=-1, keepdims=True)
                acc[rows, :] = yc * lax.rsqrt(var + LN_EPS) * ln_ref[0:1, :] + ln_ref[1:2, :]

        if last:
            for r in range(n_pass):
                out_pass(r, merge_pass(r))
        else:
            merged = [merge_pass(r) for r in range(n_pass)]
            for r in range(n_pass):
                out_pass(r, merged[r])

    @pl.when(s == 0)
    def _first_head():
        @pl.when(i == 0)
        def _():
            x_copy(i).start()
        x_copy(i).wait()
        phase1(True)

    @pl.when(jnp.logical_and(s > 0, s < N_HEADS))
    def _other_heads():
        phase1(False)

    @pl.when(s == N_HEADS)
    def _first_col():
        @pl.when(i > 0)
        def _():
            out_copy(i - 1).wait()
        phase2(True, False)

        @pl.when(i + 1 < n_tiles)
        def _():
            x_copy(i + 1).start()

    @pl.when(jnp.logical_and(s > N_HEADS, s < 2 * N_HEADS - 1))
    def _other_cols():
        phase2(False, False)

    @pl.when(s == 2 * N_HEADS - 1)
    def _last_col():
        phase2(False, True)
        out_copy(i).start()

        @pl.when(i == n_tiles - 1)
        def _():
            out_copy(i).wait()


def _layer(x2d, w_in_bf, w_oa, w_ob, w_out, head_params, w_s, b_s_col, gate_bias, ln_params, *, seq_len):
    m, d = x2d.shape
    tm = TOKEN_TILE
    assert d == D_MODEL and w_in_bf.shape == (D_MODEL, 9 * D_MODEL)
    assert m % tm == 0 and seq_len % tm == 0
    assert tm % P1_ROWS == 0 and tm % P2_ROWS == 0 and P1_ROWS % CHUNK == 0
    n_tiles = m // tm
    last = N_HEADS - 1

    def head_of(s):
        return jnp.minimum(s, last)

    def col_of(s):
        return jnp.maximum(s - N_HEADS, 0)

    def w_slot(k):
        if k < len(_SLOT_P2):
            def idx(i, s, k=k):
                return (0, jnp.where(s < N_HEADS, _SLOT_P1[k] + s, _SLOT_P2[k] + col_of(s)))
        else:
            def idx(i, s, k=k):
                return (0, _SLOT_P1[k] + head_of(s))
        return pl.BlockSpec((D_MODEL, HEAD_DIM), idx)

    in_specs = [pl.BlockSpec(memory_space=pl.ANY)]
    in_specs += [w_slot(k) for k in range(len(_SLOT_P1))]
    in_specs += [
        pl.BlockSpec((D_MODEL, HEAD_DIM), lambda i, s: (0, col_of(s))),
        pl.BlockSpec((D_MODEL, HEAD_DIM), lambda i, s: (0, col_of(s))),
        pl.BlockSpec((HEAD_DIM, D_MODEL), lambda i, s: (col_of(s), 0)),
        pl.BlockSpec((N_HEADS, _HP_ROWS, HEAD_DIM), lambda i, s: (0, 0, 0)),
        pl.BlockSpec((N_HEADS, CHUNK, CHUNK), lambda i, s: (0, 0, 0)),
        pl.BlockSpec((N_HEADS, CHUNK, 1), lambda i, s: (0, 0, 0)),
        pl.BlockSpec((N_HEADS, 2, HEAD_DIM), lambda i, s: (0, 0, 0)),
        pl.BlockSpec((2, D_MODEL), lambda i, s: (0, 0)),
    ]
    kern = functools.partial(_fused_kernel, tiles_per_seq=seq_len // tm, n_tiles=n_tiles)
    return pl.pallas_call(
        kern,
        grid=(n_tiles, 2 * N_HEADS),
        in_specs=in_specs,
        out_specs=pl.BlockSpec(memory_space=pl.ANY),
        out_shape=jax.ShapeDtypeStruct((m, D_MODEL), _F32),
        scratch_shapes=[
            pltpu.VMEM((tm, D_MODEL), _F32),
            pltpu.VMEM((tm, D_MODEL), _F32),
            pltpu.VMEM((tm, D_MODEL), _BF16),
            pltpu.VMEM((N_HEADS, tm, HEAD_DIM), _BF16),
            pltpu.VMEM((N_HEADS, tm, HEAD_DIM), _BF16),
            pltpu.VMEM((N_HEADS, SUBLANES, HEAD_DIM), _F32),
            pltpu.SemaphoreType.DMA((2,)),
        ],
        compiler_params=pltpu.CompilerParams(
            dimension_semantics=("arbitrary", "arbitrary"),
            vmem_limit_bytes=VMEM_LIMIT_BYTES),
        name="fused_layer",
    )(x2d, *([w_in_bf] * len(_SLOT_P1)), w_oa, w_ob, w_out,
      head_params, w_s, b_s_col, gate_bias, ln_params)


def kernel(x, w_in, b_gate, ln_v_g, ln_v_b, w_s, b_s, conv_w, conv_b, w_oa, w_ob, w_out, ln_g, ln_b):
    bsz, seq, d = x.shape
    depth = w_in.shape[0]
    x2d = x.reshape(bsz * seq, d)
    for l in range(depth):
        head_params = jnp.concatenate(
            [ln_v_g[l][None], ln_v_b[l][None], conv_b[l][None], conv_w[l],
             jnp.zeros((_HP_ROWS - _HP_CW0 - CONV_WIDTH, d), _F32)], axis=0)
        head_params = head_params.reshape(_HP_ROWS, N_HEADS, HEAD_DIM).transpose(1, 0, 2)
        gate_bias = b_gate[l].reshape(2, N_HEADS, HEAD_DIM).transpose(1, 0, 2)
        x2d = _layer(
            x2d, w_in[l].astype(_BF16), w_oa[l], w_ob[l], w_out[l], head_params,
            w_s[l], b_s[l].reshape(N_HEADS, CHUNK, 1),
            gate_bias, jnp.stack([ln_g[l], ln_b[l]]),
            seq_len=seq)
    return x2d.reshape(bsz, seq, d)
```

```python
import functools

import jax
import jax.numpy as jnp
from jax import lax
from jax.experimental import pallas as pl
from jax.experimental.pallas import tpu as pltpu

D_MODEL = 2048
CHUNK = 128
HEAD_DIM = 256
N_HEADS = D_MODEL // HEAD_DIM
CONV_WIDTH = 3
DN_ALPHA = 2.0 ** 0.25
LN_EPS = 1e-5

SUBLANES = 8
TOKEN_TILE = 1024
P1_ROWS = 1024
P2_ROWS = 256
VMEM_LIMIT_BYTES = 64 * 1024 * 1024

_G_U, _G_V, _G_ZA, _G_XB, _G_CB, _G_BB, _G_ZB, _G_GA, _G_GB = (g * N_HEADS for g in range(9))
_SLOT_P1 = (_G_V, _G_U, _G_ZA, _G_CB, _G_XB, _G_BB, _G_ZB)
_SLOT_P2 = (_G_GA, _G_GB)
_HP_LNG, _HP_LNB, _HP_CB, _HP_CW0 = 0, 1, 2, 3
_HP_ROWS = SUBLANES
assert _HP_CW0 + CONV_WIDTH <= _HP_ROWS

_F32 = jnp.float32
_BF16 = jnp.bfloat16


def _dot(a, b):
    return jnp.dot(a, b, preferred_element_type=_F32)


def _sigmoid(t):
    return 0.5 * jnp.tanh(0.5 * t) + 0.5


def _silu(t):
    return t * _sigmoid(t)


def _fused_kernel(x_hbm, w0, w1, w2, w3, w4, w5, w6, woa, wob, wout_ref,
                  hp_ref, ws_ref, bs_ref, bg_ref, ln_ref,
                  o_hbm,
                  xbuf, acc, xbf_ref, ya_ref, yb_ref, hcarry_ref, sem,
                  *, tiles_per_seq, n_tiles):
    i = pl.program_id(0)
    s = pl.program_id(1)
    tm = xbuf.shape[0]

    def x_copy(tile):
        return pltpu.make_async_copy(x_hbm.at[pl.ds(tile * tm, tm), :], xbuf, sem.at[0])

    def out_copy(tile):
        return pltpu.make_async_copy(acc, o_hbm.at[pl.ds(tile * tm, tm), :], sem.at[1])

    def phase1(first):
        wv, wu, wza, wcb, wxb, wbb, wzb = w0, w1, w2, w3, w4, w5, w6
        row = lax.broadcasted_iota(jnp.int32, (CHUNK, CHUNK), 0)
        col = lax.broadcasted_iota(jnp.int32, (CHUNK, CHUNK), 1)
        w_mix = jnp.where(row >= col, ws_ref[s], 0.0).astype(_BF16)
        bias_col = bs_ref[s]
        hp = hp_ref[s]

        @pl.when((i % tiles_per_seq) == 0)
        def _reset_history():
            hcarry_ref[s] = jnp.zeros((SUBLANES, HEAD_DIM), _F32)

        prev = hcarry_ref[s]
        for r in range(tm // P1_ROWS):
            rows = slice(r * P1_ROWS, (r + 1) * P1_ROWS)
            if first:
                xbf_ref[rows, :] = xbuf[rows, :].astype(_BF16)
            xb16 = xbf_ref[rows, :]
            v = jax.nn.gelu(_dot(xb16, wv[...]))
            u = jax.nn.gelu(_dot(xb16, wu[...]))
            za = _silu(_dot(xb16, wza[...]))
            h = _dot(xb16, wcb[...]) * _dot(xb16, wxb[...])

            mu = jnp.mean(v, axis=-1, keepdims=True)
            vc = v - mu
            var = jnp.mean(vc * vc, axis=-1, keepdims=True)
            vn = (vc * lax.rsqrt(var + LN_EPS) * hp[_HP_LNG:_HP_LNG + 1, :]
                  + hp[_HP_LNB:_HP_LNB + 1, :])
            vn16 = vn.astype(_BF16)
            mixed = jnp.concatenate(
                [_dot(w_mix, vn16[c * CHUNK:(c + 1) * CHUNK, :]) + bias_col
                 for c in range(P1_ROWS // CHUNK)], axis=0)
            ya_ref[s, rows, :] = (u * mixed * za).astype(_BF16)

            rid = lax.broadcasted_iota(jnp.int32, h.shape, 0)
            last1 = prev[SUBLANES - 1:SUBLANES, :]
            last2 = prev[SUBLANES - 2:SUBLANES - 1, :]
            h1 = jnp.where(rid == 0, last1, pltpu.roll(h, 1, 0))
            h2 = jnp.where(rid == 0, last2, jnp.where(rid == 1, last1, pltpu.roll(h, 2, 0)))
            conv = (hp[_HP_CB:_HP_CB + 1, :]
                    + hp[_HP_CW0:_HP_CW0 + 1, :] * h2
                    + hp[_HP_CW0 + 1:_HP_CW0 + 2, :] * h1
                    + hp[_HP_CW0 + 2:_HP_CW0 + 3, :] * h)
            prev = h[P1_ROWS - SUBLANES:, :]
            zb = _silu(_dot(xb16, wzb[...]))
            bb = _dot(xb16, wbb[...])
            yb_ref[s, rows, :] = (bb * (conv * zb)).astype(_BF16)
        hcarry_ref[s] = prev

    def phase2(seed, last):
        wga, wgb = w0, w1
        n_pass = tm // P2_ROWS
        gate_bias = bg_ref[s - N_HEADS]

        def merge_pass(r):
            rows = slice(r * P2_ROWS, (r + 1) * P2_ROWS)
            if seed:
                acc[rows, :] = DN_ALPHA * xbuf[rows, :]
            xb16 = xbf_ref[rows, :]
            ga = _sigmoid(_dot(xb16, wga[...]) + gate_bias[0:1, :])
            gb = _sigmoid(_dot(xb16, wgb[...]) + gate_bias[1:2, :])
            a = _dot(ya_ref[0, rows, :], woa[0:HEAD_DIM, :].astype(_BF16))
            b = _dot(yb_ref[0, rows, :], wob[0:HEAD_DIM, :].astype(_BF16))
            for k in range(1, N_HEADS):
                a += _dot(ya_ref[k, rows, :], woa[k * HEAD_DIM:(k + 1) * HEAD_DIM, :].astype(_BF16))
                b += _dot(yb_ref[k, rows, :], wob[k * HEAD_DIM:(k + 1) * HEAD_DIM, :].astype(_BF16))
            return (ga * a + gb * b).astype(_BF16)

        def out_pass(r, merged):
            rows = slice(r * P2_ROWS, (r + 1) * P2_ROWS)
            acc[rows, :] += _dot(merged, wout_ref[...].astype(_BF16))
            if last:
                y = acc[rows, :]
                mu = jnp.mean(y, axis=-1, keepdims=True)
                yc = y - mu
                var = jnp.mean(yc * yc, axis=-1, keepdims=True)
                acc[rows, :] = yc * lax.rsqrt(var + LN_EPS) * ln_ref[0:1, :] + ln_ref[1:2, :]

        if last:
            for r in range(n_pass):
                out_pass(r, merge_pass(r))
        else:
            merged = [merge_pass(r) for r in range(n_pass)]
            for r in range(n_pass):
                out_pass(r, merged[r])

    def post_norm_prev():
        for r in range(tm // P2_ROWS):
            rows = slice(r * P2_ROWS, (r + 1) * P2_ROWS)
            y = acc[rows, :]
            mu = jnp.mean(y, axis=-1, keepdims=True)
            yc = y - mu
            var = jnp.mean(yc * yc, axis=-1, keepdims=True)
            acc[rows, :] = yc * lax.rsqrt(var + LN_EPS) * ln_ref[0:1, :] + ln_ref[1:2, :]

    real = i < n_tiles

    @pl.when(jnp.logical_and(s == 0, i == 0))
    def _first_head_first_tile():
        x_copy(i).start()
        x_copy(i).wait()
        phase1(True)

    @pl.when(jnp.logical_and(s == 0, jnp.logical_and(i > 0, real)))
    def _first_head():
        x_copy(i).wait()
        post_norm_prev()
        phase1(True)
        out_copy(i - 1).start()

    @pl.when(jnp.logical_and(s == 0, i == n_tiles))
    def _finish_last_tile():
        post_norm_prev()
        out_copy(i - 1).start()
        out_copy(i - 1).wait()

    @pl.when(jnp.logical_and(real, jnp.logical_and(s > 0, s < N_HEADS)))
    def _other_heads():
        phase1(False)

    @pl.when(jnp.logical_and(real, s == N_HEADS))
    def _first_col():
        @pl.when(i > 0)
        def _():
            out_copy(i - 1).wait()
        phase2(True, False)

        @pl.when(i + 1 < n_tiles)
        def _():
            x_copy(i + 1).start()

    @pl.when(jnp.logical_and(real, s > N_HEADS))
    def _other_cols():
        phase2(False, False)


def _layer(x2d, w_in_bf, w_oa, w_ob, w_out, head_params, w_s, b_s_col, gate_bias, ln_params, *, seq_len):
    m, d = x2d.shape
    tm = TOKEN_TILE
    assert d == D_MODEL and w_in_bf.shape == (D_MODEL, 9 * D_MODEL)
    assert m % tm == 0 and seq_len % tm == 0
    assert tm % P1_ROWS == 0 and tm % P2_ROWS == 0 and P1_ROWS % CHUNK == 0
    n_tiles = m // tm
    last = N_HEADS - 1

    last_step = 2 * N_HEADS - 1

    def head_of(s):
        return jnp.minimum(s, last)

    def col_of(s):
        return jnp.maximum(s - N_HEADS, 0)

    def col_eff(i, s):
        return col_of(jnp.where(i == n_tiles, last_step, s))

    def w_slot(k):
        if k < len(_SLOT_P2):
            def idx(i, s, k=k):
                s = jnp.where(i == n_tiles, last_step, s)
                return (0, jnp.where(s < N_HEADS, _SLOT_P1[k] + s, _SLOT_P2[k] + col_of(s)))
        else:
            def idx(i, s, k=k):
                s = jnp.where(i == n_tiles, last_step, s)
                return (0, _SLOT_P1[k] + head_of(s))
        return pl.BlockSpec((D_MODEL, HEAD_DIM), idx)

    in_specs = [pl.BlockSpec(memory_space=pl.ANY)]
    in_specs += [w_slot(k) for k in range(len(_SLOT_P1))]
    in_specs += [
        pl.BlockSpec((D_MODEL, HEAD_DIM), lambda i, s: (0, col_eff(i, s))),
        pl.BlockSpec((D_MODEL, HEAD_DIM), lambda i, s: (0, col_eff(i, s))),
        pl.BlockSpec((HEAD_DIM, D_MODEL), lambda i, s: (col_eff(i, s), 0)),
        pl.BlockSpec((N_HEADS, _HP_ROWS, HEAD_DIM), lambda i, s: (0, 0, 0)),
        pl.BlockSpec((N_HEADS, CHUNK, CHUNK), lambda i, s: (0, 0, 0)),
        pl.BlockSpec((N_HEADS, CHUNK, 1), lambda i, s: (0, 0, 0)),
        pl.BlockSpec((N_HEADS, 2, HEAD_DIM), lambda i, s: (0, 0, 0)),
        pl.BlockSpec((2, D_MODEL), lambda i, s: (0, 0)),
    ]
    kern = functools.partial(_fused_kernel, tiles_per_seq=seq_len // tm, n_tiles=n_tiles)
    return pl.pallas_call(
        kern,
        grid=(n_tiles + 1, 2 * N_HEADS),
        in_specs=in_specs,
        out_specs=pl.BlockSpec(memory_space=pl.ANY),
        out_shape=jax.ShapeDtypeStruct((m, D_MODEL), _F32),
        scratch_shapes=[
            pltpu.VMEM((tm, D_MODEL), _F32),
            pltpu.VMEM((tm, D_MODEL), _F32),
            pltpu.VMEM((tm, D_MODEL), _BF16),
            pltpu.VMEM((N_HEADS, tm, HEAD_DIM), _BF16),
            pltpu.VMEM((N_HEADS, tm, HEAD_DIM), _BF16),
            pltpu.VMEM((N_HEADS, SUBLANES, HEAD_DIM), _F32),
            pltpu.SemaphoreType.DMA((2,)),
        ],
        compiler_params=pltpu.CompilerParams(
            dimension_semantics=("arbitrary", "arbitrary"),
            vmem_limit_bytes=VMEM_LIMIT_BYTES),
        name="fused_layer",
    )(x2d, *([w_in_bf] * len(_SLOT_P1)), w_oa, w_ob, w_out,
      head_params, w_s, b_s_col, gate_bias, ln_params)


def kernel(x, w_in, b_gate, ln_v_g, ln_v_b, w_s, b_s, conv_w, conv_b, w_oa, w_ob, w_out, ln_g, ln_b):
    bsz, seq, d = x.shape
    depth = w_in.shape[0]
    x2d = x.reshape(bsz * seq, d)
    for l in range(depth):
        head_params = jnp.concatenate(
            [ln_v_g[l][None], ln_v_b[l][None], conv_b[l][None], conv_w[l],
             jnp.zeros((_HP_ROWS - _HP_CW0 - CONV_WIDTH, d), _F32)], axis=0)
        head_params = head_params.reshape(_HP_ROWS, N_HEADS, HEAD_DIM).transpose(1, 0, 2)
        gate_bias = b_gate[l].reshape(2, N_HEADS, HEAD_DIM).transpose(1, 0, 2)
        x2d = _layer(
            x2d, w_in[l].astype(_BF16), w_oa[l], w_ob[l], w_out[l], head_params,
            w_s[l], b_s[l].reshape(N_HEADS, CHUNK, 1),
            gate_bias, jnp.stack([ln_g[l], ln_b[l]]),
            seq_len=seq)
    return x2d.reshape(bsz, seq, d)
```

```python
import functools

import jax
import jax.numpy as jnp
from jax import lax
from jax.experimental import pallas as pl
from jax.experimental.pallas import tpu as pltpu

D_MODEL = 2048
CHUNK = 128
HEAD_DIM = 256
N_HEADS = D_MODEL // HEAD_DIM
CONV_WIDTH = 3
DN_ALPHA = 2.0 ** 0.25
LN_EPS = 1e-5

SUBLANES = 8
TOKEN_TILE = 1024
P1_ROWS = 1024
P2_ROWS = 256
VMEM_LIMIT_BYTES = 64 * 1024 * 1024

_G_U, _G_V, _G_ZA, _G_XB, _G_CB, _G_BB, _G_ZB, _G_GA, _G_GB = (g * N_HEADS for g in range(9))
_SLOT_P1 = (_G_V, _G_U, _G_ZA, _G_CB, _G_XB, _G_BB, _G_ZB)
_SLOT_P2 = (_G_GA, _G_GB)
_HP_LNG, _HP_LNB, _HP_CB, _HP_CW0 = 0, 1, 2, 3
_HP_ROWS = SUBLANES
assert _HP_CW0 + CONV_WIDTH <= _HP_ROWS

_F32 = jnp.float32
_BF16 = jnp.bfloat16


def _dot(a, b):
    return jnp.dot(a, b, preferred_element_type=_F32)


def _sigmoid(t):
    return 0.5 * jnp.tanh(0.5 * t) + 0.5


def _silu(t):
    return t * _sigmoid(t)


def _fused_kernel(x_hbm, w0, w1, w2, w3, w4, w5, w6, woa, wob, wout_ref,
                  hp_ref, ws_ref, bs_ref, bg_ref, ln_ref,
                  o_hbm,
                  xbuf, acc, xbf_ref, ya_ref, yb_ref, hcarry_ref, sem,
                  *, tiles_per_seq, n_tiles):
    i = pl.program_id(0)
    s = pl.program_id(1)
    tm = xbuf.shape[0]

    def x_copy(tile):
        return pltpu.make_async_copy(x_hbm.at[pl.ds(tile * tm, tm), :], xbuf, sem.at[0])

    def out_copy(tile):
        return pltpu.make_async_copy(acc, o_hbm.at[pl.ds(tile * tm, tm), :], sem.at[1])

    def phase1(first):
        wv, wu, wza, wcb, wxb, wbb, wzb = w0, w1, w2, w3, w4, w5, w6
        row = lax.broadcasted_iota(jnp.int32, (CHUNK, CHUNK), 0)
        col = lax.broadcasted_iota(jnp.int32, (CHUNK, CHUNK), 1)
        w_mix = jnp.where(row >= col, ws_ref[s], 0.0).astype(_BF16)
        bias_col = bs_ref[s]
        hp = hp_ref[s]

        @pl.when((i % tiles_per_seq) == 0)
        def _reset_history():
            hcarry_ref[s] = jnp.zeros((SUBLANES, HEAD_DIM), _F32)

        prev = hcarry_ref[s]
        for r in range(tm // P1_ROWS):
            rows = slice(r * P1_ROWS, (r + 1) * P1_ROWS)
            if first:
                xbf_ref[rows, :] = xbuf[rows, :].astype(_BF16)
            xb16 = xbf_ref[rows, :]
            v = jax.nn.gelu(_dot(xb16, wv[...]))
            u = jax.nn.gelu(_dot(xb16, wu[...]))
            za = _silu(_dot(xb16, wza[...]))
            h = _dot(xb16, wcb[...]) * _dot(xb16, wxb[...])

            mu = jnp.mean(v, axis=-1, keepdims=True)
            vc = v - mu
            var = jnp.mean(vc * vc, axis=-1, keepdims=True)
            vn = (vc * lax.rsqrt(var + LN_EPS) * hp[_HP_LNG:_HP_LNG + 1, :]
                  + hp[_HP_LNB:_HP_LNB + 1, :])
            vn16 = vn.astype(_BF16)
            mixed = jnp.concatenate(
                [_dot(w_mix, vn16[c * CHUNK:(c + 1) * CHUNK, :]) + bias_col
                 for c in range(P1_ROWS // CHUNK)], axis=0)
            ya_ref[s, rows, :] = (u * mixed * za).astype(_BF16)

            rid = lax.broadcasted_iota(jnp.int32, h.shape, 0)
            last1 = prev[SUBLANES - 1:SUBLANES, :]
            last2 = prev[SUBLANES - 2:SUBLANES - 1, :]
            h1 = jnp.where(rid == 0, last1, pltpu.roll(h, 1, 0))
            h2 = jnp.where(rid == 0, last2, jnp.where(rid == 1, last1, pltpu.roll(h, 2, 0)))
            conv = (hp[_HP_CB:_HP_CB + 1, :]
                    + hp[_HP_CW0:_HP_CW0 + 1, :] * h2
                    + hp[_HP_CW0 + 1:_HP_CW0 + 2, :] * h1
                    + hp[_HP_CW0 + 2:_HP_CW0 + 3, :] * h)
            prev = h[P1_ROWS - SUBLANES:, :]
            zb = _silu(_dot(xb16, wzb[...]))
            bb = _dot(xb16, wbb[...])
            yb_ref[s, rows, :] = (bb * (conv * zb)).astype(_BF16)
        hcarry_ref[s] = prev

    def phase2(seed, last):
        wga, wgb = w0, w1
        n_pass = tm // P2_ROWS
        gate_bias = bg_ref[s - N_HEADS]

        def merge_pass(r):
            rows = slice(r * P2_ROWS, (r + 1) * P2_ROWS)
            if seed:
                acc[rows, :] = DN_ALPHA * xbuf[rows, :]
            xb16 = xbf_ref[rows, :]
            ga = _sigmoid(_dot(xb16, wga[...]) + gate_bias[0:1, :])
            gb = _sigmoid(_dot(xb16, wgb[...]) + gate_bias[1:2, :])
            a = _dot(ya_ref[0, rows, :], woa[0:HEAD_DIM, :].astype(_BF16))
            b = _dot(yb_ref[0, rows, :], wob[0:HEAD_DIM, :].astype(_BF16))
            for k in range(1, N_HEADS):
                a += _dot(ya_ref[k, rows, :], woa[k * HEAD_DIM:(k + 1) * HEAD_DIM, :].astype(_BF16))
                b += _dot(yb_ref[k, rows, :], wob[k * HEAD_DIM:(k + 1) * HEAD_DIM, :].astype(_BF16))
            return (ga * a + gb * b).astype(_BF16)

        def out_pass(r, merged):
            rows = slice(r * P2_ROWS, (r + 1) * P2_ROWS)
            acc[rows, :] += _dot(merged, wout_ref[...].astype(_BF16))
            if last:
                y = acc[rows, :]
                mu = jnp.mean(y, axis=-1, keepdims=True)
                yc = y - mu
                var = jnp.mean(yc * yc, axis=-1, keepdims=True)
                acc[rows, :] = yc * lax.rsqrt(var + LN_EPS) * ln_ref[0:1, :] + ln_ref[1:2, :]

        if last:
            for r in range(n_pass):
                out_pass(r, merge_pass(r))
        else:
            merged = [merge_pass(r) for r in range(n_pass)]
            for r in range(n_pass):
                out_pass(r, merged[r])

    @pl.when(s == 0)
    def _first_head():
        @pl.when(i == 0)
        def _():
            x_copy(i).start()
        x_copy(i).wait()
        phase1(True)

    @pl.when(jnp.logical_and(s > 0, s < N_HEADS))
    def _other_heads():
        phase1(False)

    @pl.when(s == N_HEADS)
    def _first_col():
        @pl.when(i > 0)
        def _():
            out_copy(i - 1).wait()
        phase2(True, False)

        @pl.when(i + 1 < n_tiles)
        def _():
            x_copy(i + 1).start(priority=1)

    @pl.when(jnp.logical_and(s > N_HEADS, s < 2 * N_HEADS - 1))
    def _other_cols():
        phase2(False, False)

    @pl.when(s == 2 * N_HEADS - 1)
    def _last_col():
        phase2(False, True)
        out_copy(i).start(priority=1)

        @pl.when(i == n_tiles - 1)
        def _():
            out_copy(i).wait()


def _layer(x2d, w_in_bf, w_oa, w_ob, w_out, head_params, w_s, b_s_col, gate_bias, ln_params, *, seq_len):
    m, d = x2d.shape
    tm = TOKEN_TILE
    assert d == D_MODEL and w_in_bf.shape == (D_MODEL, 9 * D_MODEL)
    assert m % tm == 0 and seq_len % tm == 0
    assert tm % P1_ROWS == 0 and tm % P2_ROWS == 0 and P1_ROWS % CHUNK == 0
    n_tiles = m // tm
    last = N_HEADS - 1

    def head_of(s):
        return jnp.minimum(s, last)

    def col_of(s):
        return jnp.maximum(s - N_HEADS, 0)

    def w_slot(k):
        if k < len(_SLOT_P2):
            def idx(i, s, k=k):
                return (0, jnp.where(s < N_HEADS, _SLOT_P1[k] + s, _SLOT_P2[k] + col_of(s)))
        else:
            def idx(i, s, k=k):
                return (0, _SLOT_P1[k] + head_of(s))
        return pl.BlockSpec((D_MODEL, HEAD_DIM), idx)

    in_specs = [pl.BlockSpec(memory_space=pl.ANY)]
    in_specs += [w_slot(k) for k in range(len(_SLOT_P1))]
    in_specs += [
        pl.BlockSpec((D_MODEL, HEAD_DIM), lambda i, s: (0, col_of(s))),
        pl.BlockSpec((D_MODEL, HEAD_DIM), lambda i, s: (0, col_of(s))),
        pl.BlockSpec((HEAD_DIM, D_MODEL), lambda i, s: (col_of(s), 0)),
        pl.BlockSpec((N_HEADS, _HP_ROWS, HEAD_DIM), lambda i, s: (0, 0, 0)),
        pl.BlockSpec((N_HEADS, CHUNK, CHUNK), lambda i, s: (0, 0, 0)),
        pl.BlockSpec((N_HEADS, CHUNK, 1), lambda i, s: (0, 0, 0)),
        pl.BlockSpec((N_HEADS, 2, HEAD_DIM), lambda i, s: (0, 0, 0)),
        pl.BlockSpec((2, D_MODEL), lambda i, s: (0, 0)),
    ]
    kern = functools.partial(_fused_kernel, tiles_per_seq=seq_len // tm, n_tiles=n_tiles)
    return pl.pallas_call(
        kern,
        grid=(n_tiles, 2 * N_HEADS),
        in_specs=in_specs,
        out_specs=pl.BlockSpec(memory_space=pl.ANY),
        out_shape=jax.ShapeDtypeStruct((m, D_MODEL), _F32),
        scratch_shapes=[
            pltpu.VMEM((tm, D_MODEL), _F32),
            pltpu.VMEM((tm, D_MODEL), _F32),
            pltpu.VMEM((tm, D_MODEL), _BF16),
            pltpu.VMEM((N_HEADS, tm, HEAD_DIM), _BF16),
            pltpu.VMEM((N_HEADS, tm, HEAD_DIM), _BF16),
            pltpu.VMEM((N_HEADS, SUBLANES, HEAD_DIM), _F32),
            pltpu.SemaphoreType.DMA((2,)),
        ],
        compiler_params=pltpu.CompilerParams(
            dimension_semantics=("arbitrary", "arbitrary"),
            vmem_limit_bytes=VMEM_LIMIT_BYTES),
        name="fused_layer",
    )(x2d, *([w_in_bf] * len(_SLOT_P1)), w_oa, w_ob, w_out,
      head_params, w_s, b_s_col, gate_bias, ln_params)


def kernel(x, w_in, b_gate, ln_v_g, ln_v_b, w_s, b_s, conv_w, conv_b, w_oa, w_ob, w_out, ln_g, ln_b):
    bsz, seq, d = x.shape
    depth = w_in.shape[0]
    x2d = x.reshape(bsz * seq, d)
    for l in range(depth):
        head_params = jnp.concatenate(
            [ln_v_g[l][None], ln_v_b[l][None], conv_b[l][None], conv_w[l],
             jnp.zeros((_HP_ROWS - _HP_CW0 - CONV_WIDTH, d), _F32)], axis=0)
        head_params = head_params.reshape(_HP_ROWS, N_HEADS, HEAD_DIM).transpose(1, 0, 2)
        gate_bias = b_gate[l].reshape(2, N_HEADS, HEAD_DIM).transpose(1, 0, 2)
        x2d = _layer(
            x2d, w_in[l].astype(_BF16), w_oa[l], w_ob[l], w_out[l], head_params,
            w_s[l], b_s[l].reshape(N_HEADS, CHUNK, 1),
            gate_bias, jnp.stack([ln_g[l], ln_b[l]]),
            seq_len=seq)
    return x2d.reshape(bsz, seq, d)
```
